```python
import math
import jax, jax.numpy as jnp
from jax import lax
import numpy as np


D_MODEL = 1024
BATCH = 16
SEQ = 4096
DEPTH = 2
DEC_BATCH = 8
DEC_SEQ = 2048
PAST_LEN = 128

N_MIXERS = 2
N_ATTN_LAYERS = (DEPTH + 1) // 2
N_FNET_LAYERS = DEPTH // 2
DA_HEADS = 8
DA_HEAD_DIM = D_MODEL // DA_HEADS // 2
ROPE_THETA = 10000.0
Q_BLOCK = 128
FNET_GROUPS = 4
FNET_GROUP_DIM = D_MODEL // FNET_GROUPS
MEM_TOKENS = 256
XA_HEADS = 4
XA_HEAD_DIM = D_MODEL // XA_HEADS
D_FF = 2816
CONV_WIDTH = 3
NORM_EPS = 1e-6
SUBLN_EPS = 1e-5

kernel_name = "hybrid_diffattn_fnet_encoder"


def rmsnorm(x, g, eps=NORM_EPS):
    xf = x.astype(jnp.float32)
    y = xf * lax.rsqrt(jnp.mean(xf * xf, axis=-1, keepdims=True) + eps)
    return (y * g.astype(jnp.float32)).astype(x.dtype)


def rope_tables(S):
    half = DA_HEAD_DIM // 2
    inv_freq = ROPE_THETA ** (-jnp.arange(0, half, dtype=jnp.float32) * 2.0 / DA_HEAD_DIM)
    ang = jnp.arange(S, dtype=jnp.float32)[:, None] * inv_freq[None, :]
    return jnp.cos(ang), jnp.sin(ang)


def apply_rope(x, cos, sin):
    half = DA_HEAD_DIM // 2
    xf = x.astype(jnp.float32)
    x1, x2 = xf[..., :half], xf[..., half:]
    c = cos[:, None, None, :]
    s = sin[:, None, None, :]
    out = jnp.concatenate([x1 * c - x2 * s, x2 * c + x1 * s], axis=-1)
    return out.astype(x.dtype)


def diff_attention(xn, w_qkv, lam_q1, lam_k1, lam_q2, lam_k2, subln_g, w_o, lambda_init, cos, sin):
    B, S, _ = xn.shape
    qkv = xn @ w_qkv
    q, k, v = jnp.split(qkv, 3, axis=-1)
    q = apply_rope(q.reshape(B, S, DA_HEADS, 2, DA_HEAD_DIM), cos, sin) * (DA_HEAD_DIM ** -0.5)
    k = apply_rope(k.reshape(B, S, DA_HEADS, 2, DA_HEAD_DIM), cos, sin)
    v = v.reshape(B, S, DA_HEADS, 2 * DA_HEAD_DIM)
    f32 = jnp.float32
    lam = (jnp.exp(jnp.sum(lam_q1.astype(f32) * lam_k1.astype(f32)))
           - jnp.exp(jnp.sum(lam_q2.astype(f32) * lam_k2.astype(f32)))
           + lambda_init)
    nb = S // Q_BLOCK
    qb = q.reshape(B, nb, Q_BLOCK, DA_HEADS, 2, DA_HEAD_DIM).transpose(1, 0, 2, 3, 4, 5)

    def block(qblk):
        s = jnp.einsum("bqhcd,bkhcd->bchqk", qblk, k).astype(f32)
        p = jax.nn.softmax(s, axis=-1)
        a = p[:, 0] - lam * p[:, 1]
        return jnp.einsum("bhqk,bkhe->bqhe", a.astype(v.dtype), v)

    o = lax.map(block, qb)
    o = o.transpose(1, 0, 2, 3, 4).reshape(B, S, DA_HEADS, 2 * DA_HEAD_DIM)
    o = rmsnorm(o, subln_g, SUBLN_EPS) * (1.0 - lambda_init)
    return o.reshape(B, S, D_MODEL) @ w_o


def fourier_mix(xn, w_in, w_out):
    B, S, _ = xn.shape
    u = (xn @ w_in).reshape(B, S, FNET_GROUPS, FNET_GROUP_DIM)
    f = jnp.fft.fft2(u.astype(jnp.float32), axes=(1, 3), norm="ortho").real
    return f.astype(xn.dtype).reshape(B, S, D_MODEL) @ w_out


def cross_attention(xn, memn, w_q, w_kv, w_o):
    B, S, _ = xn.shape
    M = memn.shape[1]
    q = (xn @ w_q).reshape(B, S, XA_HEADS, XA_HEAD_DIM) * (XA_HEAD_DIM ** -0.5)
    kv = (memn @ w_kv).reshape(B, M, 2, XA_HEADS, XA_HEAD_DIM)
    k, v = kv[:, :, 0], kv[:, :, 1]
    s = jnp.einsum("bqhd,bmhd->bhqm", q, k).astype(jnp.float32)
    p = jax.nn.softmax(s, axis=-1).astype(v.dtype)
    o = jnp.einsum("bhqm,bmhd->bqhd", p, v).reshape(B, S, D_MODEL)
    return o @ w_o


def dwconv3(h, w, b):
    hp = jnp.pad(h, ((0, 0), (1, 1), (0, 0)))
    return hp[:, :-2] * w[0] + hp[:, 1:-1] * w[1] + hp[:, 2:] * w[2] + b


def conv_glu(xn, w_up, conv_w, conv_b, w_down):
    gate, val = jnp.split(xn @ w_up, 2, axis=-1)
    gate = dwconv3(gate, conv_w, conv_b)
    return (jax.nn.gelu(gate) * val) @ w_down


def trunk(x, mem, norm_mix_g, norm_xattn_g, norm_mem_g, norm_ffn_g, final_norm_g,
          attn_w_qkv, attn_lambda_q1, attn_lambda_k1, attn_lambda_q2, attn_lambda_k2,
          attn_subln_g, attn_w_o, fnet_w_in, fnet_w_out,
          xattn_w_q, xattn_w_kv, xattn_w_o,
          ffn_w_up, ffn_conv_w, ffn_conv_b, ffn_w_down):
    S = x.shape[1]
    cos, sin = rope_tables(S)
    for i in range(DEPTH):
        h = rmsnorm(x, norm_mix_g[i])
        j = i // N_MIXERS
        if i % N_MIXERS == 0:
            lambda_init = 0.8 - 0.6 * math.exp(-0.3 * i)
            x = x + diff_attention(h, attn_w_qkv[j], attn_lambda_q1[j], attn_lambda_k1[j],
                                   attn_lambda_q2[j], attn_lambda_k2[j], attn_subln_g[j],
                                   attn_w_o[j], lambda_init, cos, sin)
        else:
            x = x + fourier_mix(h, fnet_w_in[j], fnet_w_out[j])
        x = x + cross_attention(rmsnorm(x, norm_xattn_g[i]), rmsnorm(mem, norm_mem_g[i]),
                                xattn_w_q[i], xattn_w_kv[i], xattn_w_o[i])
        x = x + conv_glu(rmsnorm(x, norm_ffn_g[i]), ffn_w_up[i], ffn_conv_w[i],
                         ffn_conv_b[i], ffn_w_down[i])
    return rmsnorm(x, final_norm_g)


def setup_inputs(seed: int = 0) -> dict:
    key = jax.random.key(seed)
    ks = jax.random.split(key, 28)
    D, F = D_MODEL, D_FF

    def w(k, shape, fan_in):
        return jax.random.normal(k, shape, jnp.float32) * (fan_in ** -0.5)

    def gain(k, shape):
        return 1.0 + 0.05 * jax.random.normal(k, shape, jnp.float32)

    return {
        "x_prompt": jax.random.normal(ks[0], (BATCH, SEQ, D), jnp.float32),
        "x_sample": jax.random.normal(ks[1], (DEC_BATCH, DEC_SEQ, D), jnp.float32),
        "mem_prompt": jax.random.normal(ks[2], (BATCH, MEM_TOKENS, D), jnp.float32),
        "mem_sample": jax.random.normal(ks[3], (DEC_BATCH, MEM_TOKENS, D), jnp.float32),
        "norm_mix_g": gain(ks[4], (DEPTH, D)),
        "norm_xattn_g": gain(ks[5], (DEPTH, D)),
        "norm_mem_g": gain(ks[6], (DEPTH, D)),
        "norm_ffn_g": gain(ks[7], (DEPTH, D)),
        "final_norm_g": gain(ks[8], (D,)),
        "attn_w_qkv": w(ks[9], (N_ATTN_LAYERS, D, 3 * D), D),
        "attn_lambda_q1": 0.1 * jax.random.normal(ks[10], (N_ATTN_LAYERS, DA_HEAD_DIM), jnp.float32),
        "attn_lambda_k1": 0.1 * jax.random.normal(ks[11], (N_ATTN_LAYERS, DA_HEAD_DIM), jnp.float32),
        "attn_lambda_q2": 0.1 * jax.random.normal(ks[12], (N_ATTN_LAYERS, DA_HEAD_DIM), jnp.float32),
        "attn_lambda_k2": 0.1 * jax.random.normal(ks[13], (N_ATTN_LAYERS, DA_HEAD_DIM), jnp.float32),
        "attn_subln_g": gain(ks[14], (N_ATTN_LAYERS, 2 * DA_HEAD_DIM)),
        "attn_w_o": w(ks[15], (N_ATTN_LAYERS, D, D), D),
        "fnet_w_in": w(ks[16], (N_FNET_LAYERS, D, D), D),
        "fnet_w_out": w(ks[17], (N_FNET_LAYERS, D, D), D),
        "xattn_w_q": w(ks[18], (DEPTH, D, D), D),
        "xattn_w_kv": w(ks[19], (DEPTH, D, 2 * D), D),
        "xattn_w_o": w(ks[20], (DEPTH, D, D), D),
        "ffn_w_up": w(ks[21], (DEPTH, D, 2 * F), D),
        "ffn_conv_w": w(ks[22], (DEPTH, CONV_WIDTH, F), CONV_WIDTH),
        "ffn_conv_b": 0.02 * jax.random.normal(ks[23], (DEPTH, F), jnp.float32),
        "ffn_w_down": w(ks[24], (DEPTH, F, D), F),
    }


def reference(x_prompt, x_sample, mem_prompt, mem_sample,
              norm_mix_g, norm_xattn_g, norm_mem_g, norm_ffn_g, final_norm_g,
              attn_w_qkv, attn_lambda_q1, attn_lambda_k1, attn_lambda_q2, attn_lambda_k2,
              attn_subln_g, attn_w_o, fnet_w_in, fnet_w_out,
              xattn_w_q, xattn_w_kv, xattn_w_o,
              ffn_w_up, ffn_conv_w, ffn_conv_b, ffn_w_down):
    weights = (norm_mix_g, norm_xattn_g, norm_mem_g, norm_ffn_g, final_norm_g,
               attn_w_qkv, attn_lambda_q1, attn_lambda_k1, attn_lambda_q2, attn_lambda_k2,
               attn_subln_g, attn_w_o, fnet_w_in, fnet_w_out,
               xattn_w_q, xattn_w_kv, xattn_w_o,
               ffn_w_up, ffn_conv_w, ffn_conv_b, ffn_w_down)
    y_prompt = trunk(x_prompt, mem_prompt, *weights)
    y_sample = trunk(x_sample, mem_sample, *weights)
    return (y_prompt, y_sample)
```

```python
import functools
import math

import jax
import jax.numpy as jnp
from jax import lax
from jax.experimental import pallas as pl
from jax.experimental.pallas import tpu as pltpu

D_MODEL = 1024
DEPTH = 2
N_MIXERS = 2
DA_HEADS = 8
DA_HEAD_DIM = 64
DA_V_DIM = 2 * DA_HEAD_DIM
ROPE_THETA = 10000.0
FNET_GROUPS = 4
FNET_GROUP_DIM = D_MODEL // FNET_GROUPS
XA_HEADS = 4
XA_HEAD_DIM = D_MODEL // XA_HEADS
D_FF = 2816
NORM_EPS = 1e-6
SUBLN_EPS = 1e-5

LANES = 128
SUBLANES = 8
VMEM_LIMIT_BYTES = 56 * 1024 * 1024
FFN_CHUNK = 256

F32 = jnp.float32
BF16 = jnp.bfloat16


def _params(*semantics):
    return pltpu.CompilerParams(dimension_semantics=semantics, vmem_limit_bytes=VMEM_LIMIT_BYTES)


def _resident(shape):
    return pl.BlockSpec(shape, lambda *_: (0,) * len(shape), pipeline_mode=pl.Buffered(1))


def _rms(x, g, eps):
    return x * lax.rsqrt(jnp.mean(x * x, axis=-1, keepdims=True) + eps) * g


def _dot(a, b):
    return jnp.dot(a, b, preferred_element_type=F32)


def _token_tile(seq, want):
    tile = min(seq, want)
    assert seq % tile == 0 and tile % SUBLANES == 0
    return tile


def _qkv_kernel(x_ref, g_ref, w_ref, cos_ref, sin_ref, o_ref, *, chunk):
    xn = _rms(x_ref[...], g_ref[...], NORM_EPS).astype(BF16)
    cos = cos_ref[...]
    sin = sin_ref[...]
    n_rot = 2 * D_MODEL
    for c0 in range(0, 3 * D_MODEL, chunk):
        y = _dot(xn, w_ref[:, c0:c0 + chunk])
        for j in range(0, chunk, LANES):
            blk = y[:, j:j + LANES]
            if c0 + j < n_rot:
                blk = blk * cos + pltpu.roll(blk, LANES // 2, 1) * sin
            o_ref[:, c0 + j:c0 + j + LANES] = blk.astype(o_ref.dtype)


def _qkv_rope(x, g, w, cos, sin, seq, tm):
    tokens = x.shape[0]
    n_pos = seq // tm
    return pl.pallas_call(
        functools.partial(_qkv_kernel, chunk=512),
        out_shape=jax.ShapeDtypeStruct((tokens, 3 * D_MODEL), BF16),
        grid=(tokens // tm,),
        in_specs=[
            pl.BlockSpec((tm, D_MODEL), lambda i: (i, 0)),
            _resident((1, D_MODEL)),
            _resident((D_MODEL, 3 * D_MODEL)),
            pl.BlockSpec((tm, LANES), lambda i: (i % n_pos, 0)),
            pl.BlockSpec((tm, LANES), lambda i: (i % n_pos, 0)),
        ],
        out_specs=pl.BlockSpec((tm, 3 * D_MODEL), lambda i: (i, 0)),
        compiler_params=_params("parallel"),
        name="qkv_rope",
    )(x, g, w, cos, sin)


def _diff_attn_kernel(q_ref, k_ref, v_ref, lq1_ref, lk1_ref, lq2_ref, lk2_ref, g_ref, o_ref,
                      kt_ref, m_ref, l_ref, acc_ref, *, tq, tk, seq, lambda_init):
    @pl.when(pl.program_id(2) == 0)
    def _():
        kt_ref[...] = k_ref[...].astype(F32).T.astype(BF16)

    q = q_ref[...]
    lane = lax.broadcasted_iota(jnp.int32, q.shape, 1)
    first = (lane // (DA_HEAD_DIM // 2)) % 2 == 0
    zero = jnp.zeros_like(q)
    qs = jnp.concatenate([jnp.where(first, q, zero), jnp.where(first, zero, q)], axis=0)

    m_ref[...] = jnp.full(m_ref.shape, -jnp.inf, F32)
    l_ref[...] = jnp.zeros(l_ref.shape, F32)
    acc_ref[...] = jnp.zeros(acc_ref.shape, F32)

    def step(c, carry):
        start = pl.multiple_of(c * tk, tk)
        s = _dot(qs, kt_ref[:, pl.ds(start, tk)])
        m_old = m_ref[...]
        m_new = jnp.maximum(m_old, jnp.max(s, axis=-1, keepdims=True))
        alpha = jnp.exp(m_old - m_new)
        p = jnp.exp(s - m_new)
        l_ref[...] = alpha * l_ref[...] + jnp.sum(p, axis=-1, keepdims=True)
        acc_ref[...] = alpha * acc_ref[...] + _dot(p.astype(BF16), v_ref[pl.ds(start, tk), :])
        m_ref[...] = m_new
        return carry

    lax.fori_loop(0, seq // tk, step, 0)

    lam = (jnp.exp(jnp.sum(lq1_ref[...] * lk1_ref[...], axis=-1, keepdims=True))
           - jnp.exp(jnp.sum(lq2_ref[...] * lk2_ref[...], axis=-1, keepdims=True))
           + lambda_init)
    o = acc_ref[...] / l_ref[...]
    o = o[:tq] - lam * o[tq:]
    o = _rms(o, g_ref[...], SUBLN_EPS) * (1.0 - lambda_init)
    o_ref[...] = o.astype(o_ref.dtype)


def _diff_attention(qkv, lq1, lk1, lq2, lk2, subln_g, batch, seq, lambda_init, tq, tk):
    n_q = seq // tq
    vec = _resident((1, DA_HEAD_DIM))
    return pl.pallas_call(
        functools.partial(_diff_attn_kernel, tq=tq, tk=tk, seq=seq, lambda_init=lambda_init),
        out_shape=jax.ShapeDtypeStruct((batch * seq, D_MODEL), BF16),
        grid=(batch, DA_HEADS, n_q),
        in_specs=[
            pl.BlockSpec((tq, DA_V_DIM), lambda b, h, i: (b * n_q + i, h)),
            pl.BlockSpec((seq, DA_V_DIM), lambda b, h, i: (b, DA_HEADS + h)),
            pl.BlockSpec((seq, DA_V_DIM), lambda b, h, i: (b, 2 * DA_HEADS + h)),
            vec, vec, vec, vec,
            _resident((1, DA_V_DIM)),
        ],
        out_specs=pl.BlockSpec((tq, DA_V_DIM), lambda b, h, i: (b * n_q + i, h)),
        scratch_shapes=[
            pltpu.VMEM((DA_V_DIM, seq), BF16),
            pltpu.VMEM((2 * tq, 1), F32),
            pltpu.VMEM((2 * tq, 1), F32),
            pltpu.VMEM((2 * tq, DA_V_DIM), F32),
        ],
        compiler_params=_params("parallel", "parallel", "arbitrary"),
        name="diff_attention",
    )(qkv, qkv, qkv, lq1, lk1, lq2, lk2, subln_g)


def _fnet_in_kernel(x_ref, g_ref, w_ref, dft_ref, o_ref):
    xn = _rms(x_ref[...], g_ref[...], NORM_EPS).astype(BF16)
    u = _dot(xn, w_ref[...]).astype(BF16)
    dft = dft_ref[...]
    for grp in range(FNET_GROUPS):
        lo = grp * FNET_GROUP_DIM
        ab = _dot(u[:, lo:lo + FNET_GROUP_DIM], dft)
        o_ref[:, lo:lo + FNET_GROUP_DIM] = ab[:, :FNET_GROUP_DIM].astype(o_ref.dtype)
        o_ref[:, D_MODEL + lo:D_MODEL + lo + FNET_GROUP_DIM] = ab[:, FNET_GROUP_DIM:].astype(o_ref.dtype)


def _fnet_in(x, g, w, dft, tm):
    tokens = x.shape[0]
    return pl.pallas_call(
        _fnet_in_kernel,
        out_shape=jax.ShapeDtypeStruct((tokens, 2 * D_MODEL), BF16),
        grid=(tokens // tm,),
        in_specs=[
            pl.BlockSpec((tm, D_MODEL), lambda i: (i, 0)),
            _resident((1, D_MODEL)),
            _resident((D_MODEL, D_MODEL)),
            _resident((FNET_GROUP_DIM, 2 * FNET_GROUP_DIM)),
        ],
        out_specs=pl.BlockSpec((tm, 2 * D_MODEL), lambda i: (i, 0)),
        compiler_params=_params("parallel"),
        name="fnet_in",
    )(x, g, w, dft)


def _seq_dft_kernel(cos_ref, nsin_ref, ab_ref, o_ref, acc_ref, *, scale):
    k = pl.program_id(2)

    @pl.when(k == 0)
    def _():
        acc_ref[...] = jnp.zeros(acc_ref.shape, F32)

    acc_ref[...] += (_dot(cos_ref[...], ab_ref[:, :D_MODEL])
                     + _dot(nsin_ref[...], ab_ref[:, D_MODEL:]))

    @pl.when(k == pl.num_programs(2) - 1)
    def _():
        o_ref[...] = (acc_ref[...] * scale).astype(o_ref.dtype)


def _seq_dft(ab, cos, nsin, batch, seq, tm, tk):
    n_k = seq // tk
    scale = 1.0 / math.sqrt(seq * FNET_GROUP_DIM)
    return pl.pallas_call(
        functools.partial(_seq_dft_kernel, scale=scale),
        out_shape=jax.ShapeDtypeStruct((batch * seq, D_MODEL), BF16),
        grid=(batch, seq // tm, n_k),
        in_specs=[
            pl.BlockSpec((tm, tk), lambda b, i, k: (i, k)),
            pl.BlockSpec((tm, tk), lambda b, i, k: (i, k)),
            pl.BlockSpec((tk, 2 * D_MODEL), lambda b, i, k: (b * n_k + k, 0)),
        ],
        out_specs=pl.BlockSpec((tm, D_MODEL), lambda b, i, k: (b * (seq // tm) + i, 0)),
        scratch_shapes=[pltpu.VMEM((tm, D_MODEL), F32)],
        compiler_params=_params("parallel", "parallel", "arbitrary"),
        name="seq_dft",
    )(cos, nsin, ab)


def _mem_kv_kernel(m_ref, g_ref, w_ref, o_ref):
    mn = _rms(m_ref[...], g_ref[...], NORM_EPS).astype(BF16)
    o_ref[...] = _dot(mn, w_ref[...]).astype(o_ref.dtype)


def _mem_kv(mem, g, w, tm):
    rows = mem.shape[0]
    return pl.pallas_call(
        _mem_kv_kernel,
        out_shape=jax.ShapeDtypeStruct((rows, 2 * D_MODEL), BF16),
        grid=(rows // tm,),
        in_specs=[
            pl.BlockSpec((tm, D_MODEL), lambda i: (i, 0)),
            _resident((1, D_MODEL)),
            _resident((D_MODEL, 2 * D_MODEL)),
        ],
        out_specs=pl.BlockSpec((tm, 2 * D_MODEL), lambda i: (i, 0)),
        compiler_params=_params("parallel"),
        name="mem_kv",
    )(mem, g, w)


def _post_mixer_kernel(x_ref, a_ref, wm_ref, g_ref, wq_ref, kv_ref, wo_ref, o_ref, heads_ref):
    x1 = x_ref[...] + _dot(a_ref[...], wm_ref[...])
    xn = _rms(x1, g_ref[...], NORM_EPS).astype(BF16)
    q = _dot(xn, wq_ref[...]).astype(BF16)
    for h in range(XA_HEADS):
        lo = h * XA_HEAD_DIM
        kh = kv_ref[:, lo:lo + XA_HEAD_DIM]
        vh = kv_ref[:, D_MODEL + lo:D_MODEL + lo + XA_HEAD_DIM]
        s = lax.dot_general(q[:, lo:lo + XA_HEAD_DIM], kh, (((1,), (1,)), ((), ())),
                            preferred_element_type=F32)
        p = jnp.exp(s - jnp.max(s, axis=-1, keepdims=True))
        denom = jnp.sum(p, axis=-1, keepdims=True)
        heads_ref[:, lo:lo + XA_HEAD_DIM] = (_dot(p.astype(BF16), vh) / denom).astype(BF16)
    o_ref[...] = x1 + _dot(heads_ref[...], wo_ref[...])


def _post_mixer(x, a, w_mix_out, g, w_q, kv, w_o, seq, mem_tokens, tm):
    tokens = x.shape[0]
    per_seq = seq // tm
    square = _resident((D_MODEL, D_MODEL))
    return pl.pallas_call(
        _post_mixer_kernel,
        out_shape=jax.ShapeDtypeStruct((tokens, D_MODEL), F32),
        grid=(tokens // tm,),
        in_specs=[
            pl.BlockSpec((tm, D_MODEL), lambda i: (i, 0)),
            pl.BlockSpec((tm, D_MODEL), lambda i: (i, 0)),
            square,
            _resident((1, D_MODEL)),
            square,
            pl.BlockSpec((mem_tokens, 2 * D_MODEL), lambda i: (i // per_seq, 0)),
            square,
        ],
        out_specs=pl.BlockSpec((tm, D_MODEL), lambda i: (i, 0)),
        scratch_shapes=[pltpu.VMEM((tm, D_MODEL), BF16)],
        compiler_params=_params("parallel"),
        name="post_mixer_xattn",
    )(x, a, w_mix_out, g, w_q, kv, w_o)


def _gelu_tanh(x):
    return x * (0.5 * (1.0 + jnp.tanh(math.sqrt(2.0 / math.pi) * (x + 0.044715 * (x * x * x)))))


def _ffn_kernel(x_ref, prev_ref, next_ref, g_ref, wup_ref, cw_ref, cb_ref, wdn_ref, gf_ref, o_ref,
                h_ref, *, tm, per_seq, final_norm):
    i = pl.program_id(0)
    x = x_ref[...]
    halo = SUBLANES
    xcat = jnp.concatenate([prev_ref[...], x, next_ref[...]], axis=0)
    xn = _rms(xcat, g_ref[...], NORM_EPS).astype(BF16)
    xn_mid = xn[halo:halo + tm]
    row = lax.broadcasted_iota(jnp.int32, (tm, 1), 0)
    keep_prev = jnp.logical_or(row > 0, i % per_seq != 0)
    keep_next = jnp.logical_or(row < tm - 1, i % per_seq != per_seq - 1)
    rows = tm + 2 * halo
    for c0 in range(0, D_FF, FFN_CHUNK):
        gate = _dot(xn, wup_ref[:, c0:c0 + FFN_CHUNK])
        val = _dot(xn_mid, wup_ref[:, D_FF + c0:D_FF + c0 + FFN_CHUNK])
        g_prev = jnp.where(keep_prev, pltpu.roll(gate, 1, 0)[halo:halo + tm], 0.0)
        g_next = jnp.where(keep_next, pltpu.roll(gate, rows - 1, 0)[halo:halo + tm], 0.0)
        g_mid = gate[halo:halo + tm]
        cw = cw_ref[:, c0:c0 + FFN_CHUNK]
        conv = g_prev * cw[0:1] + g_mid * cw[1:2] + g_next * cw[2:3] + cb_ref[:, c0:c0 + FFN_CHUNK]
        h_ref[:, c0:c0 + FFN_CHUNK] = (_gelu_tanh(conv) * val).astype(BF16)
    y = x + _dot(h_ref[...], wdn_ref[...])
    if final_norm:
        y = _rms(y, gf_ref[...], NORM_EPS)
    o_ref[...] = y


def _ffn(x, g, w_up, conv_w, conv_b, w_down, g_final, seq, tm, final_norm):
    tokens = x.shape[0]
    per_seq = seq // tm
    groups = tm // SUBLANES
    last_group = tokens // SUBLANES - 1
    return pl.pallas_call(
        functools.partial(_ffn_kernel, tm=tm, per_seq=per_seq, final_norm=final_norm),
        out_shape=jax.ShapeDtypeStruct((tokens, D_MODEL), F32),
        grid=(tokens // tm,),
        in_specs=[
            pl.BlockSpec((tm, D_MODEL), lambda i: (i, 0)),
            pl.BlockSpec((SUBLANES, D_MODEL), lambda i: (jnp.maximum(i * groups - 1, 0), 0)),
            pl.BlockSpec((SUBLANES, D_MODEL), lambda i: (jnp.minimum((i + 1) * groups, last_group), 0)),
            _resident((1, D_MODEL)),
            _resident((D_MODEL, 2 * D_FF)),
            _resident((3, D_FF)),
            _resident((1, D_FF)),
            _resident((D_FF, D_MODEL)),
            _resident((1, D_MODEL)),
        ],
        out_specs=pl.BlockSpec((tm, D_MODEL), lambda i: (i, 0)),
        scratch_shapes=[pltpu.VMEM((tm, D_FF), BF16)],
        compiler_params=_params("parallel"),
        name="ffn_convglu",
    )(x, x, x, g, w_up, conv_w, conv_b, w_down, g_final)


def _rope_tables(seq):
    half = DA_HEAD_DIM // 2
    inv_freq = ROPE_THETA ** (-jnp.arange(0, half, dtype=F32) * 2.0 / DA_HEAD_DIM)
    ang = jnp.arange(seq, dtype=F32)[:, None] * inv_freq[None, :]
    cos, sin = jnp.cos(ang), jnp.sin(ang)
    return jnp.tile(cos, (1, 4)), jnp.concatenate([-sin, -sin, sin, sin], axis=1)


def _dft_tables(n):
    idx = jnp.arange(n, dtype=jnp.int32)
    ang = ((idx[:, None] * idx[None, :]) % n).astype(F32) * (2.0 * math.pi / n)
    return jnp.cos(ang), jnp.sin(ang)


def _rotary_head_layout(w):
    d_in = w.shape[0]
    w = w.reshape(d_in, DA_HEADS, 2, 2, DA_HEAD_DIM // 2)
    return w.transpose(0, 1, 3, 2, 4).reshape(d_in, D_MODEL)


def _qkv_weight(w_qkv):
    wq, wk, wv = jnp.split(w_qkv, 3, axis=1)
    wq = _rotary_head_layout(wq) * (DA_HEAD_DIM ** -0.5)
    return jnp.concatenate([wq, _rotary_head_layout(wk), wv], axis=1).astype(BF16)


def _trunk(x, mem, p):
    batch, seq, _ = x.shape
    mem_tokens = mem.shape[1]
    tokens = batch * seq
    tm = _token_tile(seq, 512)
    x = x.reshape(tokens, D_MODEL)
    mem = mem.reshape(batch * mem_tokens, D_MODEL)

    for layer in range(DEPTH):
        j = layer // N_MIXERS
        g_mix = p["norm_mix_g"][layer][None]
        if layer % N_MIXERS == 0:
            lambda_init = 0.8 - 0.6 * math.exp(-0.3 * layer)
            cos, sin = _rope_tables(seq)
            qkv = _qkv_rope(x, g_mix, p["w_qkv"][j], cos, sin, seq, tm)
            a = _diff_attention(qkv, p["lq1"][j][None], p["lk1"][j][None], p["lq2"][j][None],
                                p["lk2"][j][None], p["subln_g"][j][None], batch, seq, lambda_init,
                                tq=_token_tile(seq, 256), tk=_token_tile(seq, 512))
            w_mix_out = p["attn_w_o"][j]
        else:
            ccos, csin = _dft_tables(FNET_GROUP_DIM)
            ab = _fnet_in(x, g_mix, p["fnet_w_in"][j], jnp.concatenate([ccos, csin], axis=1).astype(BF16), tm)
            scos, ssin = _dft_tables(seq)
            a = _seq_dft(ab, scos.astype(BF16), (-ssin).astype(BF16), batch, seq,
                         tm=_token_tile(seq, 1024), tk=_token_tile(seq, 1024))
            w_mix_out = p["fnet_w_out"][j]
        kv = _mem_kv(mem, p["norm_mem_g"][layer][None], p["xattn_w_kv"][layer], _token_tile(mem_tokens, 256))
        x = _post_mixer(x, a, w_mix_out, p["norm_xattn_g"][layer][None], p["xattn_w_q"][layer], kv,
                        p["xattn_w_o"][layer], seq, mem_tokens, tm)
        x = _ffn(x, p["norm_ffn_g"][layer][None], p["ffn_w_up"][layer], p["ffn_conv_w"][layer],
                 p["ffn_conv_b"][layer][None], p["ffn_w_down"][layer], p["final_norm_g"][None],
                 seq, tm, final_norm=(layer == DEPTH - 1))
    return x.reshape(batch, seq, D_MODEL)


def kernel(x_prompt, x_sample, mem_prompt, mem_sample, norm_mix_g, norm_xattn_g, norm_mem_g, norm_ffn_g, final_norm_g, attn_w_qkv, attn_lambda_q1, attn_lambda_k1, attn_lambda_q2, attn_lambda_k2, attn_subln_g, attn_w_o, fnet_w_in, fnet_w_out, xattn_w_q, xattn_w_kv, xattn_w_o, ffn_w_up, ffn_conv_w, ffn_conv_b, ffn_w_down):
    p = {
        "norm_mix_g": norm_mix_g, "norm_xattn_g": norm_xattn_g, "norm_mem_g": norm_mem_g,
        "norm_ffn_g": norm_ffn_g, "final_norm_g": final_norm_g,
        "w_qkv": jnp.stack([_qkv_weight(w) for w in attn_w_qkv]),
        "lq1": attn_lambda_q1, "lk1": attn_lambda_k1, "lq2": attn_lambda_q2, "lk2": attn_lambda_k2,
        "subln_g": attn_subln_g,
        "attn_w_o": attn_w_o.astype(BF16),
        "fnet_w_in": fnet_w_in.astype(BF16), "fnet_w_out": fnet_w_out.astype(BF16),
        "xattn_w_q": (xattn_w_q * (XA_HEAD_DIM ** -0.5)).astype(BF16),
        "xattn_w_kv": xattn_w_kv.astype(BF16), "xattn_w_o": xattn_w_o.astype(BF16),
        "ffn_w_up": ffn_w_up.astype(BF16), "ffn_conv_w": ffn_conv_w, "ffn_conv_b": ffn_conv_b,
        "ffn_w_down": ffn_w_down.astype(BF16),
    }
    return (_trunk(x_prompt, mem_prompt, p), _trunk(x_sample, mem_sample, p))
```

```python
import functools
import math

import jax
import jax.numpy as jnp
from jax import lax
from jax.experimental import pallas as pl
from jax.experimental.pallas import tpu as pltpu

D_MODEL = 1024
DEPTH = 2
N_MIXERS = 2
DA_HEADS = 8
DA_HEAD_DIM = 64
DA_V_DIM = 2 * DA_HEAD_DIM
ROPE_THETA = 10000.0
FNET_GROUPS = 4
FNET_GROUP_DIM = D_MODEL // FNET_GROUPS
XA_HEADS = 4
XA_HEAD_DIM = D_MODEL // XA_HEADS
D_FF = 2816
NORM_EPS = 1e-6
SUBLN_EPS = 1e-5

LANES = 128
SUBLANES = 8
VMEM_LIMIT_BYTES = 56 * 1024 * 1024
FFN_CHUNK = 256

F32 = jnp.float32
BF16 = jnp.bfloat16


def _params(*semantics):
    return pltpu.CompilerParams(dimension_semantics=semantics, vmem_limit_bytes=VMEM_LIMIT_BYTES)


def _resident(shape):
    return pl.BlockSpec(shape, lambda *_: (0,) * len(shape), pipeline_mode=pl.Buffered(1))


def _rms(x, g, eps):
    return x * lax.rsqrt(jnp.mean(x * x, axis=-1, keepdims=True) + eps) * g


def _dot(a, b):
    return jnp.dot(a, b, preferred_element_type=F32)


def _token_tile(seq, want):
    tile = min(seq, want)
    assert seq % tile == 0 and tile % SUBLANES == 0
    return tile


def _qkv_kernel(x_ref, g_ref, w_ref, cos_ref, sin_ref, o_ref, *, chunk):
    xn = _rms(x_ref[...], g_ref[...], NORM_EPS).astype(BF16)
    for c0 in range(0, 3 * D_MODEL, chunk):
        y = _dot(xn, w_ref[:, c0:c0 + chunk])
        for j in range(0, chunk, LANES):
            blk = y[:, j:j + LANES]
            part = (c0 + j) // D_MODEL
            if part < 2:
                cos = cos_ref[:, part * LANES:(part + 1) * LANES]
                sin = sin_ref[:, part * LANES:(part + 1) * LANES]
                blk = blk * cos + pltpu.roll(blk, LANES // 2, 1) * sin
            o_ref[:, c0 + j:c0 + j + LANES] = blk.astype(o_ref.dtype)


def _qkv_rope(x, g, w, cos, sin, seq, tm):
    tokens = x.shape[0]
    n_pos = seq // tm
    return pl.pallas_call(
        functools.partial(_qkv_kernel, chunk=512),
        out_shape=jax.ShapeDtypeStruct((tokens, 3 * D_MODEL), BF16),
        grid=(tokens // tm,),
        in_specs=[
            pl.BlockSpec((tm, D_MODEL), lambda i: (i, 0)),
            _resident((1, D_MODEL)),
            _resident((D_MODEL, 3 * D_MODEL)),
            pl.BlockSpec((tm, 2 * LANES), lambda i: (i % n_pos, 0)),
            pl.BlockSpec((tm, 2 * LANES), lambda i: (i % n_pos, 0)),
        ],
        out_specs=pl.BlockSpec((tm, 3 * D_MODEL), lambda i: (i, 0)),
        compiler_params=_params("parallel"),
        name="qkv_rope",
    )(x, g, w, cos, sin)


def _diff_attn_kernel(q_ref, k_ref, v_ref, lq1_ref, lk1_ref, lq2_ref, lk2_ref, g_ref, o_ref,
                      kt_ref, s_ref, m_ref, mx_ref, l_ref, acc_ref, *, tq, tk, seq, n_q, n_tiles, lambda_init):
    t = pl.program_id(0)

    @pl.when(t == 0)
    def _():
        s_ref[...] = jnp.zeros(s_ref.shape, F32)
        m_ref[...] = jnp.zeros(m_ref.shape, F32)

    @pl.when(jnp.logical_and(t % n_q == 0, t < n_tiles))
    def _():
        kt_ref[...] = k_ref[...].astype(F32).T.astype(BF16)

    q = q_ref[...]
    lane = lax.broadcasted_iota(jnp.int32, q.shape, 1)
    first = (lane // (DA_HEAD_DIM // 2)) % 2 == 0
    zero = jnp.zeros_like(q)
    qs = jnp.concatenate([jnp.where(first, q, zero), jnp.where(first, zero, q)], axis=0)

    mx_ref[...] = jnp.full(mx_ref.shape, -jnp.inf, F32)
    l_ref[...] = jnp.zeros(l_ref.shape, F32)
    acc_ref[...] = jnp.zeros(acc_ref.shape, F32)

    def chunk(c, carry):
        off = pl.multiple_of(c * tk, tk)
        s_new = _dot(qs, kt_ref[:, pl.ds(off, tk)])
        s_old = s_ref[:, pl.ds(off, tk)]
        m_old = m_ref[...]
        mx = mx_ref[...]
        lsum = l_ref[...]
        ps = []
        for j in range(0, tk, LANES):
            mx = jnp.maximum(mx, s_new[:, j:j + LANES])
            pj = jnp.exp2(s_old[:, j:j + LANES] - m_old)
            lsum = lsum + pj
            ps.append(pj.astype(BF16))
        mx_ref[...] = mx
        l_ref[...] = lsum
        s_ref[:, pl.ds(off, tk)] = s_new
        acc_ref[...] += _dot(jnp.concatenate(ps, axis=1), v_ref[pl.ds(off, tk), :])
        return carry

    lax.fori_loop(0, seq // tk, chunk, 0)

    lam = (jnp.exp(jnp.sum(lq1_ref[...] * lk1_ref[...], axis=-1, keepdims=True))
           - jnp.exp(jnp.sum(lq2_ref[...] * lk2_ref[...], axis=-1, keepdims=True))
           + lambda_init)
    o = acc_ref[...] / jnp.sum(l_ref[...], axis=-1, keepdims=True)
    o = o[:tq] - lam * o[tq:]
    o = _rms(o, g_ref[...], SUBLN_EPS) * (1.0 - lambda_init)
    o_ref[...] = o.astype(o_ref.dtype)
    m_ref[...] = jnp.broadcast_to(jnp.max(mx_ref[...], axis=-1, keepdims=True), m_ref.shape)


def _diff_attention(qkv, lq1, lk1, lq2, lk2, subln_g, batch, seq, lambda_init, tq, tk):
    n_q = seq // tq
    n_tiles = batch * DA_HEADS * n_q

    def scored(t):
        return jnp.minimum(t, n_tiles - 1)

    def applied(t):
        return jnp.maximum(t - 1, 0)

    def head_of(tile):
        return tile // (DA_HEADS * n_q), (tile // n_q) % DA_HEADS

    def q_block(tile):
        b, h = head_of(tile)
        return b * n_q + tile % n_q, h

    def k_block(t):
        b, h = head_of(scored(t))
        return b, DA_HEADS + h

    def v_block(t):
        b, h = head_of(applied(t))
        return b, 2 * DA_HEADS + h

    vec = _resident((1, DA_HEAD_DIM))
    rows = 2 * tq
    return pl.pallas_call(
        functools.partial(_diff_attn_kernel, tq=tq, tk=tk, seq=seq, n_q=n_q, n_tiles=n_tiles,
                          lambda_init=lambda_init),
        out_shape=jax.ShapeDtypeStruct((batch * seq, D_MODEL), BF16),
        grid=(n_tiles + 1,),
        in_specs=[
            pl.BlockSpec((tq, DA_V_DIM), lambda t: q_block(scored(t))),
            pl.BlockSpec((seq, DA_V_DIM), k_block),
            pl.BlockSpec((seq, DA_V_DIM), v_block),
            vec, vec, vec, vec,
            _resident((1, DA_V_DIM)),
        ],
        out_specs=pl.BlockSpec((tq, DA_V_DIM), lambda t: q_block(applied(t))),
        scratch_shapes=[
            pltpu.VMEM((DA_V_DIM, seq), BF16),
            pltpu.VMEM((rows, seq), F32),
            pltpu.VMEM((rows, LANES), F32),
            pltpu.VMEM((rows, LANES), F32),
            pltpu.VMEM((rows, LANES), F32),
            pltpu.VMEM((rows, DA_V_DIM), F32),
        ],
        compiler_params=_params("arbitrary"),
        name="diff_attention",
    )(qkv, qkv, qkv, lq1, lk1, lq2, lk2, subln_g)


def _fnet_in_kernel(x_ref, g_ref, w_ref, dft_ref, o_ref):
    xn = _rms(x_ref[...], g_ref[...], NORM_EPS).astype(BF16)
    u = _dot(xn, w_ref[...]).astype(BF16)
    dft = dft_ref[...]
    for grp in range(FNET_GROUPS):
        lo = grp * FNET_GROUP_DIM
        ab = _dot(u[:, lo:lo + FNET_GROUP_DIM], dft)
        o_ref[:, lo:lo + FNET_GROUP_DIM] = ab[:, :FNET_GROUP_DIM].astype(o_ref.dtype)
        o_ref[:, D_MODEL + lo:D_MODEL + lo + FNET_GROUP_DIM] = ab[:, FNET_GROUP_DIM:].astype(o_ref.dtype)


def _fnet_in(x, g, w, dft, tm):
    tokens = x.shape[0]
    return pl.pallas_call(
        _fnet_in_kernel,
        out_shape=jax.ShapeDtypeStruct((tokens, 2 * D_MODEL), BF16),
        grid=(tokens // tm,),
        in_specs=[
            pl.BlockSpec((tm, D_MODEL), lambda i: (i, 0)),
            _resident((1, D_MODEL)),
            _resident((D_MODEL, D_MODEL)),
            _resident((FNET_GROUP_DIM, 2 * FNET_GROUP_DIM)),
        ],
        out_specs=pl.BlockSpec((tm, 2 * D_MODEL), lambda i: (i, 0)),
        compiler_params=_params("parallel"),
        name="fnet_in",
    )(x, g, w, dft)


def _seq_dft_kernel(cos_ref, nsin_ref, ab_ref, o_ref, acc_ref, *, scale):
    k = pl.program_id(2)

    @pl.when(k == 0)
    def _():
        acc_ref[...] = jnp.zeros(acc_ref.shape, F32)

    acc_ref[...] += (_dot(cos_ref[...], ab_ref[:, :D_MODEL])
                     + _dot(nsin_ref[...], ab_ref[:, D_MODEL:]))

    @pl.when(k == pl.num_programs(2) - 1)
    def _():
        o_ref[...] = (acc_ref[...] * scale).astype(o_ref.dtype)


def _seq_dft(ab, cos, nsin, batch, seq, tm, tk):
    n_k = seq // tk
    scale = 1.0 / math.sqrt(seq * FNET_GROUP_DIM)
    return pl.pallas_call(
        functools.partial(_seq_dft_kernel, scale=scale),
        out_shape=jax.ShapeDtypeStruct((batch * seq, D_MODEL), BF16),
        grid=(batch, seq // tm, n_k),
        in_specs=[
            pl.BlockSpec((tm, tk), lambda b, i, k: (i, k)),
            pl.BlockSpec((tm, tk), lambda b, i, k: (i, k)),
            pl.BlockSpec((tk, 2 * D_MODEL), lambda b, i, k: (b * n_k + k, 0)),
        ],
        out_specs=pl.BlockSpec((tm, D_MODEL), lambda b, i, k: (b * (seq // tm) + i, 0)),
        scratch_shapes=[pltpu.VMEM((tm, D_MODEL), F32)],
        compiler_params=_params("parallel", "parallel", "arbitrary"),
        name="seq_dft",
    )(cos, nsin, ab)


def _mem_kv_kernel(m_ref, g_ref, w_ref, o_ref):
    mn = _rms(m_ref[...], g_ref[...], NORM_EPS).astype(BF16)
    o_ref[...] = _dot(mn, w_ref[...]).astype(o_ref.dtype)


def _mem_kv(mem, g, w, tm):
    rows = mem.shape[0]
    return pl.pallas_call(
        _mem_kv_kernel,
        out_shape=jax.ShapeDtypeStruct((rows, 2 * D_MODEL), BF16),
        grid=(rows // tm,),
        in_specs=[
            pl.BlockSpec((tm, D_MODEL), lambda i: (i, 0)),
            _resident((1, D_MODEL)),
            _resident((D_MODEL, 2 * D_MODEL)),
        ],
        out_specs=pl.BlockSpec((tm, 2 * D_MODEL), lambda i: (i, 0)),
        compiler_params=_params("parallel"),
        name="mem_kv",
    )(mem, g, w)


def _post_mixer_kernel(x_ref, a_ref, wm_ref, g_ref, wq_ref, kv_ref, wo_ref, o_ref, heads_ref):
    x1 = x_ref[...] + _dot(a_ref[...], wm_ref[...])
    xn = _rms(x1, g_ref[...], NORM_EPS).astype(BF16)
    q = _dot(xn, wq_ref[...]).astype(BF16)
    for h in range(XA_HEADS):
        lo = h * XA_HEAD_DIM
        kh = kv_ref[:, lo:lo + XA_HEAD_DIM]
        vh = kv_ref[:, D_MODEL + lo:D_MODEL + lo + XA_HEAD_DIM]
        s = lax.dot_general(q[:, lo:lo + XA_HEAD_DIM], kh, (((1,), (1,)), ((), ())),
                            preferred_element_type=F32)
        p = jnp.exp(s - jnp.max(s, axis=-1, keepdims=True))
        denom = jnp.sum(p, axis=-1, keepdims=True)
        heads_ref[:, lo:lo + XA_HEAD_DIM] = (_dot(p.astype(BF16), vh) / denom).astype(BF16)
    o_ref[...] = x1 + _dot(heads_ref[...], wo_ref[...])


def _post_mixer(x, a, w_mix_out, g, w_q, kv, w_o, seq, mem_tokens, tm):
    tokens = x.shape[0]
    per_seq = seq // tm
    square = _resident((D_MODEL, D_MODEL))
    return pl.pallas_call(
        _post_mixer_kernel,
        out_shape=jax.ShapeDtypeStruct((tokens, D_MODEL), F32),
        grid=(tokens // tm,),
        in_specs=[
            pl.BlockSpec((tm, D_MODEL), lambda i: (i, 0)),
            pl.BlockSpec((tm, D_MODEL), lambda i: (i, 0)),
            square,
            _resident((1, D_MODEL)),
            square,
            pl.BlockSpec((mem_tokens, 2 * D_MODEL), lambda i: (i // per_seq, 0)),
            square,
        ],
        out_specs=pl.BlockSpec((tm, D_MODEL), lambda i: (i, 0)),
        scratch_shapes=[pltpu.VMEM((tm, D_MODEL), BF16)],
        compiler_params=_params("parallel"),
        name="post_mixer_xattn",
    )(x, a, w_mix_out, g, w_q, kv, w_o)


def _gelu_tanh(x):
    return x * (0.5 * (1.0 + jnp.tanh(math.sqrt(2.0 / math.pi) * (x + 0.044715 * (x * x * x)))))


def _ffn_kernel(x_ref, prev_ref, next_ref, g_ref, wup_ref, cw_ref, cb_ref, wdn_ref, gf_ref, o_ref,
                h_ref, *, tm, per_seq, final_norm):
    i = pl.program_id(0)
    x = x_ref[...]
    halo = SUBLANES
    xcat = jnp.concatenate([prev_ref[...], x, next_ref[...]], axis=0)
    xn = _rms(xcat, g_ref[...], NORM_EPS).astype(BF16)
    xn_mid = xn[halo:halo + tm]
    row = lax.broadcasted_iota(jnp.int32, (tm, 1), 0)
    keep_prev = jnp.logical_or(row > 0, i % per_seq != 0)
    keep_next = jnp.logical_or(row < tm - 1, i % per_seq != per_seq - 1)
    rows = tm + 2 * halo
    for c0 in range(0, D_FF, FFN_CHUNK):
        gate = _dot(xn, wup_ref[:, c0:c0 + FFN_CHUNK])
        val = _dot(xn_mid, wup_ref[:, D_FF + c0:D_FF + c0 + FFN_CHUNK])
        g_prev = jnp.where(keep_prev, pltpu.roll(gate, 1, 0)[halo:halo + tm], 0.0)
        g_next = jnp.where(keep_next, pltpu.roll(gate, rows - 1, 0)[halo:halo + tm], 0.0)
        g_mid = gate[halo:halo + tm]
        cw = cw_ref[:, c0:c0 + FFN_CHUNK]
        conv = g_prev * cw[0:1] + g_mid * cw[1:2] + g_next * cw[2:3] + cb_ref[:, c0:c0 + FFN_CHUNK]
        h_ref[:, c0:c0 + FFN_CHUNK] = (_gelu_tanh(conv) * val).astype(BF16)
    y = x + _dot(h_ref[...], wdn_ref[...])
    if final_norm:
        y = _rms(y, gf_ref[...], NORM_EPS)
    o_ref[...] = y


def _ffn(x, g, w_up, conv_w, conv_b, w_down, g_final, seq, tm, final_norm):
    tokens = x.shape[0]
    per_seq = seq // tm
    groups = tm // SUBLANES
    last_group = tokens // SUBLANES - 1
    return pl.pallas_call(
        functools.partial(_ffn_kernel, tm=tm, per_seq=per_seq, final_norm=final_norm),
        out_shape=jax.ShapeDtypeStruct((tokens, D_MODEL), F32),
        grid=(tokens // tm,),
        in_specs=[
            pl.BlockSpec((tm, D_MODEL), lambda i: (i, 0)),
            pl.BlockSpec((SUBLANES, D_MODEL), lambda i: (jnp.maximum(i * groups - 1, 0), 0)),
            pl.BlockSpec((SUBLANES, D_MODEL), lambda i: (jnp.minimum((i + 1) * groups, last_group), 0)),
            _resident((1, D_MODEL)),
            _resident((D_MODEL, 2 * D_FF)),
            _resident((3, D_FF)),
            _resident((1, D_FF)),
            _resident((D_FF, D_MODEL)),
            _resident((1, D_MODEL)),
        ],
        out_specs=pl.BlockSpec((tm, D_MODEL), lambda i: (i, 0)),
        scratch_shapes=[pltpu.VMEM((tm, D_FF), BF16)],
        compiler_params=_params("parallel"),
        name="ffn_convglu",
    )(x, x, x, g, w_up, conv_w, conv_b, w_down, g_final)


def _rope_tables(seq):
    half = DA_HEAD_DIM // 2
    inv_freq = ROPE_THETA ** (-jnp.arange(0, half, dtype=F32) * 2.0 / DA_HEAD_DIM)
    ang = jnp.arange(seq, dtype=F32)[:, None] * inv_freq[None, :]
    cos = jnp.tile(jnp.cos(ang), (1, 4))
    sin = jnp.sin(ang)
    sin = jnp.concatenate([-sin, -sin, sin, sin], axis=1)
    log2e = math.log2(math.e)
    return jnp.concatenate([cos * log2e, cos], axis=1), jnp.concatenate([sin * log2e, sin], axis=1)


def _dft_tables(n):
    idx = jnp.arange(n, dtype=jnp.int32)
    ang = ((idx[:, None] * idx[None, :]) % n).astype(F32) * (2.0 * math.pi / n)
    return jnp.cos(ang), jnp.sin(ang)


def _rotary_head_layout(w):
    d_in = w.shape[0]
    w = w.reshape(d_in, DA_HEADS, 2, 2, DA_HEAD_DIM // 2)
    return w.transpose(0, 1, 3, 2, 4).reshape(d_in, D_MODEL)


def _qkv_weight(w_qkv):
    wq, wk, wv = jnp.split(w_qkv, 3, axis=1)
    wq = _rotary_head_layout(wq) * (DA_HEAD_DIM ** -0.5)
    return jnp.concatenate([wq, _rotary_head_layout(wk), wv], axis=1).astype(BF16)


def _trunk(x, mem, p):
    batch, seq, _ = x.shape
    mem_tokens = mem.shape[1]
    tokens = batch * seq
    tm = _token_tile(seq, 512)
    x = x.reshape(tokens, D_MODEL)
    mem = mem.reshape(batch * mem_tokens, D_MODEL)

    for layer in range(DEPTH):
        j = layer // N_MIXERS
        g_mix = p["norm_mix_g"][layer][None]
        if layer % N_MIXERS == 0:
            lambda_init = 0.8 - 0.6 * math.exp(-0.3 * layer)
            cos, sin = _rope_tables(seq)
            qkv = _qkv_rope(x, g_mix, p["w_qkv"][j], cos, sin, seq, tm)
            a = _diff_attention(qkv, p["lq1"][j][None], p["lk1"][j][None], p["lq2"][j][None],
                                p["lk2"][j][None], p["subln_g"][j][None], batch, seq, lambda_init,
                                tq=_token_tile(seq, 256), tk=_token_tile(seq, 512))
            w_mix_out = p["attn_w_o"][j]
        else:
            ccos, csin = _dft_tables(FNET_GROUP_DIM)
            ab = _fnet_in(x, g_mix, p["fnet_w_in"][j], jnp.concatenate([ccos, csin], axis=1).astype(BF16), tm)
            scos, ssin = _dft_tables(seq)
            a = _seq_dft(ab, scos.astype(BF16), (-ssin).astype(BF16), batch, seq,
                         tm=_token_tile(seq, 1024), tk=_token_tile(seq, 1024))
            w_mix_out = p["fnet_w_out"][j]
        kv = _mem_kv(mem, p["norm_mem_g"][layer][None], p["xattn_w_kv"][layer], _token_tile(mem_tokens, 256))
        x = _post_mixer(x, a, w_mix_out, p["norm_xattn_g"][layer][None], p["xattn_w_q"][layer], kv,
                        p["xattn_w_o"][layer], seq, mem_tokens, tm)
        x = _ffn(x, p["norm_ffn_g"][layer][None], p["ffn_w_up"][layer], p["ffn_conv_w"][layer],
                 p["ffn_conv_b"][layer][None], p["ffn_w_down"][layer], p["final_norm_g"][None],
                 seq, tm, final_norm=(layer == DEPTH - 1))
    return x.reshape(batch, seq, D_MODEL)


def kernel(x_prompt, x_sample, mem_prompt, mem_sample, norm_mix_g, norm_xattn_g, norm_mem_g, norm_ffn_g, final_norm_g, attn_w_qkv, attn_lambda_q1, attn_lambda_k1, attn_lambda_q2, attn_lambda_k2, attn_subln_g, attn_w_o, fnet_w_in, fnet_w_out, xattn_w_q, xattn_w_kv, xattn_w_o, ffn_w_up, ffn_conv_w, ffn_conv_b, ffn_w_down):
    p = {
        "norm_mix_g": norm_mix_g, "norm_xattn_g": norm_xattn_g, "norm_mem_g": norm_mem_g,
        "norm_ffn_g": norm_ffn_g, "final_norm_g": final_norm_g,
        "w_qkv": jnp.stack([_qkv_weight(w) for w in attn_w_qkv]),
        "lq1": attn_lambda_q1, "lk1": attn_lambda_k1, "lq2": attn_lambda_q2, "lk2": attn_lambda_k2,
        "subln_g": attn_subln_g,
        "attn_w_o": attn_w_o.astype(BF16),
        "fnet_w_in": fnet_w_in.astype(BF16), "fnet_w_out": fnet_w_out.astype(BF16),
        "xattn_w_q": (xattn_w_q * (XA_HEAD_DIM ** -0.5)).astype(BF16),
        "xattn_w_kv": xattn_w_kv.astype(BF16), "xattn_w_o": xattn_w_o.astype(BF16),
        "ffn_w_up": ffn_w_up.astype(BF16), "ffn_conv_w": ffn_conv_w, "ffn_conv_b": ffn_conv_b,
        "ffn_w_down": ffn_w_down.astype(BF16),
    }
    return (_trunk(x_prompt, mem_prompt, p), _trunk(x_sample, mem_sample, p))
```

```python
import functools
import math

import jax
import jax.numpy as jnp
from jax import lax
from jax.experimental import pallas as pl
from jax.experimental.pallas import tpu as pltpu

D_MODEL = 1024
DEPTH = 2
N_MIXERS = 2
DA_HEADS = 8
DA_HEAD_DIM = 64
DA_V_DIM = 2 * DA_HEAD_DIM
ROPE_THETA = 10000.0
FNET_GROUPS = 4
FNET_GROUP_DIM = D_MODEL // FNET_GROUPS
XA_HEADS = 4
XA_HEAD_DIM = D_MODEL // XA_HEADS
D_FF = 2816
NORM_EPS = 1e-6
SUBLN_EPS = 1e-5

LANES = 128
SUBLANES = 8
VMEM_LIMIT_BYTES = 56 * 1024 * 1024
FFN_CHUNK = 256

F32 = jnp.float32
BF16 = jnp.bfloat16


def _params(*semantics):
    return pltpu.CompilerParams(dimension_semantics=semantics, vmem_limit_bytes=VMEM_LIMIT_BYTES)


def _resident(shape):
    return pl.BlockSpec(shape, lambda *_: (0,) * len(shape), pipeline_mode=pl.Buffered(1))


def _rms(x, g, eps):
    return x * lax.rsqrt(jnp.mean(x * x, axis=-1, keepdims=True) + eps) * g


def _dot(a, b):
    return jnp.dot(a, b, preferred_element_type=F32)


def _token_tile(seq, want):
    tile = min(seq, want)
    assert seq % tile == 0 and tile % SUBLANES == 0
    return tile


def _qkv_kernel(x_ref, g_ref, w_ref, cos_ref, sin_ref, o_ref, *, chunk):
    xn = _rms(x_ref[...], g_ref[...], NORM_EPS).astype(BF16)
    for c0 in range(0, 3 * D_MODEL, chunk):
        y = _dot(xn, w_ref[:, c0:c0 + chunk])
        for j in range(0, chunk, LANES):
            blk = y[:, j:j + LANES]
            part = (c0 + j) // D_MODEL
            if part < 2:
                cos = cos_ref[:, part * LANES:(part + 1) * LANES]
                sin = sin_ref[:, part * LANES:(part + 1) * LANES]
                blk = blk * cos + pltpu.roll(blk, LANES // 2, 1) * sin
            o_ref[:, c0 + j:c0 + j + LANES] = blk.astype(o_ref.dtype)


def _qkv_rope(x, g, w, cos, sin, seq, tm):
    tokens = x.shape[0]
    n_pos = seq // tm
    return pl.pallas_call(
        functools.partial(_qkv_kernel, chunk=512),
        out_shape=jax.ShapeDtypeStruct((tokens, 3 * D_MODEL), BF16),
        grid=(tokens // tm,),
        in_specs=[
            pl.BlockSpec((tm, D_MODEL), lambda i: (i, 0)),
            _resident((1, D_MODEL)),
            _resident((D_MODEL, 3 * D_MODEL)),
            pl.BlockSpec((tm, 2 * LANES), lambda i: (i % n_pos, 0)),
            pl.BlockSpec((tm, 2 * LANES), lambda i: (i % n_pos, 0)),
        ],
        out_specs=pl.BlockSpec((tm, 3 * D_MODEL), lambda i: (i, 0)),
        compiler_params=_params("parallel"),
        name="qkv_rope",
    )(x, g, w, cos, sin)


def _diff_attn_kernel(q_ref, k_ref, v_ref, lq1_ref, lk1_ref, lq2_ref, lk2_ref, g_ref, o_ref,
                      kt_ref, s_ref, m_ref, *, tq, tk, seq, n_q, n_tiles, lambda_init):
    t = pl.program_id(0)

    @pl.when(t == 0)
    def _():
        s_ref[...] = jnp.zeros(s_ref.shape, F32)
        m_ref[...] = jnp.zeros(m_ref.shape, F32)

    @pl.when(jnp.logical_and(t % n_q == 0, t < n_tiles))
    def _():
        kt_ref[...] = k_ref[...].astype(F32).T.astype(BF16)

    q = q_ref[...]
    lane = lax.broadcasted_iota(jnp.int32, q.shape, 1)
    first = (lane // (DA_HEAD_DIM // 2)) % 2 == 0
    zero = jnp.zeros_like(q)
    qs = jnp.concatenate([jnp.where(first, q, zero), jnp.where(first, zero, q)], axis=0)

    m_old = m_ref[...]
    mx = lsum = acc = None
    for off in range(0, seq, tk):
        s_new = _dot(qs, kt_ref[:, off:off + tk])
        s_old = s_ref[:, off:off + tk]
        ps = []
        for j in range(0, tk, LANES):
            sj = s_new[:, j:j + LANES]
            mx = sj if mx is None else jnp.maximum(mx, sj)
            pj = jnp.exp2(s_old[:, j:j + LANES] - m_old)
            lsum = pj if lsum is None else lsum + pj
            ps.append(pj.astype(BF16))
        s_ref[:, off:off + tk] = s_new
        pv = _dot(jnp.concatenate(ps, axis=1), v_ref[off:off + tk, :])
        acc = pv if acc is None else acc + pv

    lam = (jnp.exp(jnp.sum(lq1_ref[...] * lk1_ref[...], axis=-1, keepdims=True))
           - jnp.exp(jnp.sum(lq2_ref[...] * lk2_ref[...], axis=-1, keepdims=True))
           + lambda_init)
    o = acc / jnp.sum(lsum, axis=-1, keepdims=True)
    o = o[:tq] - lam * o[tq:]
    o = _rms(o, g_ref[...], SUBLN_EPS) * (1.0 - lambda_init)
    o_ref[...] = o.astype(o_ref.dtype)
    m_ref[...] = jnp.broadcast_to(jnp.max(mx, axis=-1, keepdims=True), m_ref.shape)


def _diff_attention(qkv, lq1, lk1, lq2, lk2, subln_g, batch, seq, lambda_init, tq, tk):
    n_q = seq // tq
    n_tiles = batch * DA_HEADS * n_q

    def scored(t):
        return jnp.minimum(t, n_tiles - 1)

    def applied(t):
        return jnp.maximum(t - 1, 0)

    def head_of(tile):
        return tile // (DA_HEADS * n_q), (tile // n_q) % DA_HEADS

    def q_block(tile):
        b, h = head_of(tile)
        return b * n_q + tile % n_q, h

    def k_block(t):
        b, h = head_of(scored(t))
        return b, DA_HEADS + h

    def v_block(t):
        b, h = head_of(applied(t))
        return b, 2 * DA_HEADS + h

    vec = _resident((1, DA_HEAD_DIM))
    rows = 2 * tq
    return pl.pallas_call(
        functools.partial(_diff_attn_kernel, tq=tq, tk=tk, seq=seq, n_q=n_q, n_tiles=n_tiles,
                          lambda_init=lambda_init),
        out_shape=jax.ShapeDtypeStruct((batch * seq, D_MODEL), BF16),
        grid=(n_tiles + 1,),
        in_specs=[
            pl.BlockSpec((tq, DA_V_DIM), lambda t: q_block(scored(t))),
            pl.BlockSpec((seq, DA_V_DIM), k_block),
            pl.BlockSpec((seq, DA_V_DIM), v_block),
            vec, vec, vec, vec,
            _resident((1, DA_V_DIM)),
        ],
        out_specs=pl.BlockSpec((tq, DA_V_DIM), lambda t: q_block(applied(t))),
        scratch_shapes=[
            pltpu.VMEM((DA_V_DIM, seq), BF16),
            pltpu.VMEM((rows, seq), F32),
            pltpu.VMEM((rows, LANES), F32),
        ],
        compiler_params=_params("arbitrary"),
        name="diff_attention",
    )(qkv, qkv, qkv, lq1, lk1, lq2, lk2, subln_g)


def _fnet_in_kernel(x_ref, g_ref, w_ref, dft_ref, o_ref):
    xn = _rms(x_ref[...], g_ref[...], NORM_EPS).astype(BF16)
    u = _dot(xn, w_ref[...]).astype(BF16)
    dft = dft_ref[...]
    for grp in range(FNET_GROUPS):
        lo = grp * FNET_GROUP_DIM
        ab = _dot(u[:, lo:lo + FNET_GROUP_DIM], dft)
        o_ref[:, lo:lo + FNET_GROUP_DIM] = ab[:, :FNET_GROUP_DIM].astype(o_ref.dtype)
        o_ref[:, D_MODEL + lo:D_MODEL + lo + FNET_GROUP_DIM] = ab[:, FNET_GROUP_DIM:].astype(o_ref.dtype)


def _fnet_in(x, g, w, dft, tm):
    tokens = x.shape[0]
    return pl.pallas_call(
        _fnet_in_kernel,
        out_shape=jax.ShapeDtypeStruct((tokens, 2 * D_MODEL), BF16),
        grid=(tokens // tm,),
        in_specs=[
            pl.BlockSpec((tm, D_MODEL), lambda i: (i, 0)),
            _resident((1, D_MODEL)),
            _resident((D_MODEL, D_MODEL)),
            _resident((FNET_GROUP_DIM, 2 * FNET_GROUP_DIM)),
        ],
        out_specs=pl.BlockSpec((tm, 2 * D_MODEL), lambda i: (i, 0)),
        compiler_params=_params("parallel"),
        name="fnet_in",
    )(x, g, w, dft)


def _seq_dft_kernel(cos_ref, nsin_ref, ab_ref, o_ref, acc_ref, *, scale):
    k = pl.program_id(2)

    @pl.when(k == 0)
    def _():
        acc_ref[...] = jnp.zeros(acc_ref.shape, F32)

    acc_ref[...] += (_dot(cos_ref[...], ab_ref[:, :D_MODEL])
                     + _dot(nsin_ref[...], ab_ref[:, D_MODEL:]))

    @pl.when(k == pl.num_programs(2) - 1)
    def _():
        o_ref[...] = (acc_ref[...] * scale).astype(o_ref.dtype)


def _seq_dft(ab, cos, nsin, batch, seq, tm, tk):
    n_k = seq // tk
    scale = 1.0 / math.sqrt(seq * FNET_GROUP_DIM)
    return pl.pallas_call(
        functools.partial(_seq_dft_kernel, scale=scale),
        out_shape=jax.ShapeDtypeStruct((batch * seq, D_MODEL), BF16),
        grid=(batch, seq // tm, n_k),
        in_specs=[
            pl.BlockSpec((tm, tk), lambda b, i, k: (i, k)),
            pl.BlockSpec((tm, tk), lambda b, i, k: (i, k)),
            pl.BlockSpec((tk, 2 * D_MODEL), lambda b, i, k: (b * n_k + k, 0)),
        ],
        out_specs=pl.BlockSpec((tm, D_MODEL), lambda b, i, k: (b * (seq // tm) + i, 0)),
        scratch_shapes=[pltpu.VMEM((tm, D_MODEL), F32)],
        compiler_params=_params("parallel", "parallel", "arbitrary"),
        name="seq_dft",
    )(cos, nsin, ab)


def _mem_kv_kernel(m_ref, g_ref, w_ref, o_ref):
    mn = _rms(m_ref[...], g_ref[...], NORM_EPS).astype(BF16)
    o_ref[...] = _dot(mn, w_ref[...]).astype(o_ref.dtype)


def _mem_kv(mem, g, w, tm):
    rows = mem.shape[0]
    return pl.pallas_call(
        _mem_kv_kernel,
        out_shape=jax.ShapeDtypeStruct((rows, 2 * D_MODEL), BF16),
        grid=(rows // tm,),
        in_specs=[
            pl.BlockSpec((tm, D_MODEL), lambda i: (i, 0)),
            _resident((1, D_MODEL)),
            _resident((D_MODEL, 2 * D_MODEL)),
        ],
        out_specs=pl.BlockSpec((tm, 2 * D_MODEL), lambda i: (i, 0)),
        compiler_params=_params("parallel"),
        name="mem_kv",
    )(mem, g, w)


def _post_mixer_kernel(x_ref, a_ref, wm_ref, g_ref, wq_ref, kv_ref, wo_ref, o_ref, heads_ref):
    x1 = x_ref[...] + _dot(a_ref[...], wm_ref[...])
    xn = _rms(x1, g_ref[...], NORM_EPS).astype(BF16)
    q = _dot(xn, wq_ref[...]).astype(BF16)
    for h in range(XA_HEADS):
        lo = h * XA_HEAD_DIM
        kh = kv_ref[:, lo:lo + XA_HEAD_DIM]
        vh = kv_ref[:, D_MODEL + lo:D_MODEL + lo + XA_HEAD_DIM]
        s = lax.dot_general(q[:, lo:lo + XA_HEAD_DIM], kh, (((1,), (1,)), ((), ())),
                            preferred_element_type=F32)
        p = jnp.exp(s - jnp.max(s, axis=-1, keepdims=True))
        denom = jnp.sum(p, axis=-1, keepdims=True)
        heads_ref[:, lo:lo + XA_HEAD_DIM] = (_dot(p.astype(BF16), vh) / denom).astype(BF16)
    o_ref[...] = x1 + _dot(heads_ref[...], wo_ref[...])


def _post_mixer(x, a, w_mix_out, g, w_q, kv, w_o, seq, mem_tokens, tm):
    tokens = x.shape[0]
    per_seq = seq // tm
    square = _resident((D_MODEL, D_MODEL))
    return pl.pallas_call(
        _post_mixer_kernel,
        out_shape=jax.ShapeDtypeStruct((tokens, D_MODEL), F32),
        grid=(tokens // tm,),
        in_specs=[
            pl.BlockSpec((tm, D_MODEL), lambda i: (i, 0)),
            pl.BlockSpec((tm, D_MODEL), lambda i: (i, 0)),
            square,
            _resident((1, D_MODEL)),
            square,
            pl.BlockSpec((mem_tokens, 2 * D_MODEL), lambda i: (i // per_seq, 0)),
            square,
        ],
        out_specs=pl.BlockSpec((tm, D_MODEL), lambda i: (i, 0)),
        scratch_shapes=[pltpu.VMEM((tm, D_MODEL), BF16)],
        compiler_params=_params("parallel"),
        name="post_mixer_xattn",
    )(x, a, w_mix_out, g, w_q, kv, w_o)


def _gelu_tanh(x):
    return x * (0.5 * (1.0 + jnp.tanh(math.sqrt(2.0 / math.pi) * (x + 0.044715 * (x * x * x)))))


def _ffn_kernel(x_ref, prev_ref, next_ref, g_ref, wup_ref, cw_ref, cb_ref, wdn_ref, gf_ref, o_ref,
                h_ref, *, tm, per_seq, final_norm):
    i = pl.program_id(0)
    x = x_ref[...]
    halo = SUBLANES
    xcat = jnp.concatenate([prev_ref[...], x, next_ref[...]], axis=0)
    xn = _rms(xcat, g_ref[...], NORM_EPS).astype(BF16)
    xn_mid = xn[halo:halo + tm]
    row = lax.broadcasted_iota(jnp.int32, (tm, 1), 0)
    keep_prev = jnp.logical_or(row > 0, i % per_seq != 0)
    keep_next = jnp.logical_or(row < tm - 1, i % per_seq != per_seq - 1)
    rows = tm + 2 * halo
    for c0 in range(0, D_FF, FFN_CHUNK):
        gate = _dot(xn, wup_ref[:, c0:c0 + FFN_CHUNK])
        val = _dot(xn_mid, wup_ref[:, D_FF + c0:D_FF + c0 + FFN_CHUNK])
        g_prev = jnp.where(keep_prev, pltpu.roll(gate, 1, 0)[halo:halo + tm], 0.0)
        g_next = jnp.where(keep_next, pltpu.roll(gate, rows - 1, 0)[halo:halo + tm], 0.0)
        g_mid = gate[halo:halo + tm]
        cw = cw_ref[:, c0:c0 + FFN_CHUNK]
        conv = g_prev * cw[0:1] + g_mid * cw[1:2] + g_next * cw[2:3] + cb_ref[:, c0:c0 + FFN_CHUNK]
        h_ref[:, c0:c0 + FFN_CHUNK] = (_gelu_tanh(conv) * val).astype(BF16)
    y = x + _dot(h_ref[...], wdn_ref[...])
    if final_norm:
        y = _rms(y, gf_ref[...], NORM_EPS)
    o_ref[...] = y


def _ffn(x, g, w_up, conv_w, conv_b, w_down, g_final, seq, tm, final_norm):
    tokens = x.shape[0]
    per_seq = seq // tm
    groups = tm // SUBLANES
    last_group = tokens // SUBLANES - 1
    return pl.pallas_call(
        functools.partial(_ffn_kernel, tm=tm, per_seq=per_seq, final_norm=final_norm),
        out_shape=jax.ShapeDtypeStruct((tokens, D_MODEL), F32),
        grid=(tokens // tm,),
        in_specs=[
            pl.BlockSpec((tm, D_MODEL), lambda i: (i, 0)),
            pl.BlockSpec((SUBLANES, D_MODEL), lambda i: (jnp.maximum(i * groups - 1, 0), 0)),
            pl.BlockSpec((SUBLANES, D_MODEL), lambda i: (jnp.minimum((i + 1) * groups, last_group), 0)),
            _resident((1, D_MODEL)),
            _resident((D_MODEL, 2 * D_FF)),
            _resident((3, D_FF)),
            _resident((1, D_FF)),
            _resident((D_FF, D_MODEL)),
            _resident((1, D_MODEL)),
        ],
        out_specs=pl.BlockSpec((tm, D_MODEL), lambda i: (i, 0)),
        scratch_shapes=[pltpu.VMEM((tm, D_FF), BF16)],
        compiler_params=_params("parallel"),
        name="ffn_convglu",
    )(x, x, x, g, w_up, conv_w, conv_b, w_down, g_final)


def _rope_tables(seq):
    half = DA_HEAD_DIM // 2
    inv_freq = ROPE_THETA ** (-jnp.arange(0, half, dtype=F32) * 2.0 / DA_HEAD_DIM)
    ang = jnp.arange(seq, dtype=F32)[:, None] * inv_freq[None, :]
    cos = jnp.tile(jnp.cos(ang), (1, 4))
    sin = jnp.sin(ang)
    sin = jnp.concatenate([-sin, -sin, sin, sin], axis=1)
    log2e = math.log2(math.e)
    return jnp.concatenate([cos * log2e, cos], axis=1), jnp.concatenate([sin * log2e, sin], axis=1)


def _dft_tables(n):
    idx = jnp.arange(n, dtype=jnp.int32)
    ang = ((idx[:, None] * idx[None, :]) % n).astype(F32) * (2.0 * math.pi / n)
    return jnp.cos(ang), jnp.sin(ang)


def _rotary_head_layout(w):
    d_in = w.shape[0]
    w = w.reshape(d_in, DA_HEADS, 2, 2, DA_HEAD_DIM // 2)
    return w.transpose(0, 1, 3, 2, 4).reshape(d_in, D_MODEL)


def _qkv_weight(w_qkv):
    wq, wk, wv = jnp.split(w_qkv, 3, axis=1)
    wq = _rotary_head_layout(wq) * (DA_HEAD_DIM ** -0.5)
    return jnp.concatenate([wq, _rotary_head_layout(wk), wv], axis=1).astype(BF16)


def _trunk(x, mem, p):
    batch, seq, _ = x.shape
    mem_tokens = mem.shape[1]
    tokens = batch * seq
    tm = _token_tile(seq, 512)
    x = x.reshape(tokens, D_MODEL)
    mem = mem.reshape(batch * mem_tokens, D_MODEL)

    for layer in range(DEPTH):
        j = layer // N_MIXERS
        g_mix = p["norm_mix_g"][layer][None]
        if layer % N_MIXERS == 0:
            lambda_init = 0.8 - 0.6 * math.exp(-0.3 * layer)
            cos, sin = _rope_tables(seq)
            qkv = _qkv_rope(x, g_mix, p["w_qkv"][j], cos, sin, seq, tm)
            a = _diff_attention(qkv, p["lq1"][j][None], p["lk1"][j][None], p["lq2"][j][None],
                                p["lk2"][j][None], p["subln_g"][j][None], batch, seq, lambda_init,
                                tq=_token_tile(seq, 256), tk=_token_tile(seq, 512))
            w_mix_out = p["attn_w_o"][j]
        else:
            ccos, csin = _dft_tables(FNET_GROUP_DIM)
            ab = _fnet_in(x, g_mix, p["fnet_w_in"][j], jnp.concatenate([ccos, csin], axis=1).astype(BF16), tm)
            scos, ssin = _dft_tables(seq)
            a = _seq_dft(ab, scos.astype(BF16), (-ssin).astype(BF16), batch, seq,
                         tm=_token_tile(seq, 1024), tk=_token_tile(seq, 1024))
            w_mix_out = p["fnet_w_out"][j]
        kv = _mem_kv(mem, p["norm_mem_g"][layer][None], p["xattn_w_kv"][layer], _token_tile(mem_tokens, 256))
        x = _post_mixer(x, a, w_mix_out, p["norm_xattn_g"][layer][None], p["xattn_w_q"][layer], kv,
                        p["xattn_w_o"][layer], seq, mem_tokens, tm)
        x = _ffn(x, p["norm_ffn_g"][layer][None], p["ffn_w_up"][layer], p["ffn_conv_w"][layer],
                 p["ffn_conv_b"][layer][None], p["ffn_w_down"][layer], p["final_norm_g"][None],
                 seq, tm, final_norm=(layer == DEPTH - 1))
    return x.reshape(batch, seq, D_MODEL)


def kernel(x_prompt, x_sample, mem_prompt, mem_sample, norm_mix_g, norm_xattn_g, norm_mem_g, norm_ffn_g, final_norm_g, attn_w_qkv, attn_lambda_q1, attn_lambda_k1, attn_lambda_q2, attn_lambda_k2, attn_subln_g, attn_w_o, fnet_w_in, fnet_w_out, xattn_w_q, xattn_w_kv, xattn_w_o, ffn_w_up, ffn_conv_w, ffn_conv_b, ffn_w_down):
    p = {
        "norm_mix_g": norm_mix_g, "norm_xattn_g": norm_xattn_g, "norm_mem_g": norm_mem_g,
        "norm_ffn_g": norm_ffn_g, "final_norm_g": final_norm_g,
        "w_qkv": jnp.stack([_qkv_weight(w) for w in attn_w_qkv]),
        "lq1": attn_lambda_q1, "lk1": attn_lambda_k1, "lq2": attn_lambda_q2, "lk2": attn_lambda_k2,
        "subln_g": attn_subln_g,
        "attn_w_o": attn_w_o.astype(BF16),
        "fnet_w_in": fnet_w_in.astype(BF16), "fnet_w_out": fnet_w_out.astype(BF16),
        "xattn_w_q": (xattn_w_q * (XA_HEAD_DIM ** -0.5)).astype(BF16),
        "xattn_w_kv": xattn_w_kv.astype(BF16), "xattn_w_o": xattn_w_o.astype(BF16),
        "ffn_w_up": ffn_w_up.astype(BF16), "ffn_conv_w": ffn_conv_w, "ffn_conv_b": ffn_conv_b,
        "ffn_w_down": ffn_w_down.astype(BF16),
    }
    return (_trunk(x_prompt, mem_prompt, p), _trunk(x_sample, mem_sample, p))
```

```python
import functools
import math

import jax
import jax.numpy as jnp
from jax import lax
from jax.experimental import pallas as pl
from jax.experimental.pallas import tpu as pltpu

D_MODEL = 1024
DEPTH = 2
N_MIXERS = 2
DA_HEADS = 8
DA_HEAD_DIM = 64
DA_V_DIM = 2 * DA_HEAD_DIM
ROPE_THETA = 10000.0
FNET_GROUPS = 4
FNET_GROUP_DIM = D_MODEL // FNET_GROUPS
FNET_SEQ_BASE = 256
XA_HEADS = 4
XA_HEAD_DIM = D_MODEL // XA_HEADS
D_FF = 2816
NORM_EPS = 1e-6
SUBLN_EPS = 1e-5

LANES = 128
SUBLANES = 8
VMEM_LIMIT_BYTES = 56 * 1024 * 1024
FFN_CHUNK = 256

F32 = jnp.float32
BF16 = jnp.bfloat16


def _params(*semantics, flags=None):
    return pltpu.CompilerParams(dimension_semantics=semantics, vmem_limit_bytes=VMEM_LIMIT_BYTES, flags=flags)


def _resident(shape):
    return pl.BlockSpec(shape, lambda *_: (0,) * len(shape), pipeline_mode=pl.Buffered(1))


def _rms(x, g, eps):
    return x * lax.rsqrt(jnp.mean(x * x, axis=-1, keepdims=True) + eps) * g


def _dot(a, b):
    return jnp.dot(a, b, preferred_element_type=F32)


def _token_tile(seq, want):
    tile = min(seq, want)
    assert seq % tile == 0 and tile % SUBLANES == 0
    return tile


def _qkv_kernel(x_ref, g_ref, w_ref, cos_ref, sin_ref, o_ref, *, chunk):
    xn = _rms(x_ref[...], g_ref[...], NORM_EPS).astype(BF16)
    for c0 in range(0, 3 * D_MODEL, chunk):
        y = _dot(xn, w_ref[:, c0:c0 + chunk])
        for j in range(0, chunk, LANES):
            blk = y[:, j:j + LANES]
            part = (c0 + j) // D_MODEL
            if part < 2:
                cos = cos_ref[:, part * LANES:(part + 1) * LANES]
                sin = sin_ref[:, part * LANES:(part + 1) * LANES]
                blk = blk * cos + pltpu.roll(blk, LANES // 2, 1) * sin
            o_ref[:, c0 + j:c0 + j + LANES] = blk.astype(o_ref.dtype)


def _qkv_rope(x, g, w, cos, sin, seq, tm):
    tokens = x.shape[0]
    n_pos = seq // tm
    return pl.pallas_call(
        functools.partial(_qkv_kernel, chunk=512),
        out_shape=jax.ShapeDtypeStruct((tokens, 3 * D_MODEL), BF16),
        grid=(tokens // tm,),
        in_specs=[
            pl.BlockSpec((tm, D_MODEL), lambda i: (i, 0)),
            _resident((1, D_MODEL)),
            _resident((D_MODEL, 3 * D_MODEL)),
            pl.BlockSpec((tm, 2 * LANES), lambda i: (i % n_pos, 0)),
            pl.BlockSpec((tm, 2 * LANES), lambda i: (i % n_pos, 0)),
        ],
        out_specs=pl.BlockSpec((tm, 3 * D_MODEL), lambda i: (i, 0)),
        compiler_params=_params("parallel"),
        name="qkv_rope",
    )(x, g, w, cos, sin)


def _diff_attn_kernel(q_ref, k_ref, v_ref, lq1_ref, lk1_ref, lq2_ref, lk2_ref, g_ref, o_ref,
                      vt_ref, s_ref, m_ref, acc_ref, l_ref, *, tq, tk, seq, n_q, lambda_init):
    t = pl.program_id(0)
    cols = 2 * tq
    sub = SUBLANES

    @pl.when(t == 0)
    def _():
        s_ref[...] = jnp.zeros(s_ref.shape, F32)
        m_ref[...] = jnp.zeros(m_ref.shape, F32)
        acc_ref[...] = jnp.zeros(acc_ref.shape, F32)
        l_ref[...] = jnp.ones(l_ref.shape, F32)

    @pl.when(jnp.maximum(t - 1, 0) % n_q == 0)
    def _():
        vt_ref[...] = v_ref[...].astype(F32).T.astype(BF16)

    q = q_ref[...]
    lane = lax.broadcasted_iota(jnp.int32, q.shape, 1)
    first = (lane // (DA_HEAD_DIM // 2)) % 2 == 0
    zero = jnp.zeros_like(q)
    qs = jnp.concatenate([jnp.where(first, q, zero), jnp.where(first, zero, q)], axis=0)

    lam = (jnp.exp(jnp.sum(lq1_ref[...] * lk1_ref[...], axis=-1, keepdims=True))
           - jnp.exp(jnp.sum(lq2_ref[...] * lk2_ref[...], axis=-1, keepdims=True))
           + lambda_init)
    o = acc_ref[...] / jnp.sum(l_ref[...], axis=0, keepdims=True)
    o = o[:, :tq] - lam * o[:, tq:]
    o = o * lax.rsqrt(jnp.mean(o * o, axis=0, keepdims=True) + SUBLN_EPS) * g_ref[...] * (1.0 - lambda_init)
    o_ref[...] = o.T.astype(o_ref.dtype)

    m_old = m_ref[...]
    mx = lsum = acc = None
    for off in range(0, seq, tk):
        s_new = lax.dot_general(k_ref[off:off + tk, :], qs, (((1,), (1,)), ((), ())),
                                preferred_element_type=F32)
        s_old = s_ref[off:off + tk, :]
        chunk_max = jnp.max(s_new.reshape(tk // sub, sub, cols), axis=0)
        mx = chunk_max if mx is None else jnp.maximum(mx, chunk_max)
        p = jnp.exp2(s_old.reshape(tk // sub, sub, cols) - m_old)
        chunk_sum = jnp.sum(p, axis=0)
        lsum = chunk_sum if lsum is None else lsum + chunk_sum
        s_ref[off:off + tk, :] = s_new
        pv = _dot(vt_ref[:, off:off + tk], p.reshape(tk, cols).astype(BF16))
        acc = pv if acc is None else acc + pv

    acc_ref[...] = acc
    l_ref[...] = lsum
    m_ref[...] = jnp.broadcast_to(jnp.max(mx, axis=0, keepdims=True), m_ref.shape)


def _diff_attention(qkv, lq1, lk1, lq2, lk2, subln_g, batch, seq, lambda_init, tq, tk):
    n_q = seq // tq
    n_tiles = batch * DA_HEADS * n_q

    def stage_tile(t, lag):
        return jnp.clip(t - lag, 0, n_tiles - 1)

    def scored(t):
        return stage_tile(t, 0)

    def applied(t):
        return stage_tile(t, 1)

    def written(t):
        return stage_tile(t, 2)

    def head_of(tile):
        return tile // (DA_HEADS * n_q), (tile // n_q) % DA_HEADS

    def q_block(tile):
        b, h = head_of(tile)
        return b * n_q + tile % n_q, h

    def k_block(t):
        b, h = head_of(scored(t))
        return b, DA_HEADS + h

    def v_block(t):
        b, h = head_of(applied(t))
        return b, 2 * DA_HEADS + h

    vec = _resident((1, DA_HEAD_DIM))
    cols = 2 * tq
    return pl.pallas_call(
        functools.partial(_diff_attn_kernel, tq=tq, tk=tk, seq=seq, n_q=n_q, lambda_init=lambda_init),
        out_shape=jax.ShapeDtypeStruct((batch * seq, D_MODEL), BF16),
        grid=(n_tiles + 2,),
        in_specs=[
            pl.BlockSpec((tq, DA_V_DIM), lambda t: q_block(scored(t))),
            pl.BlockSpec((seq, DA_V_DIM), k_block),
            pl.BlockSpec((seq, DA_V_DIM), v_block),
            vec, vec, vec, vec,
            _resident((DA_V_DIM, 1)),
        ],
        out_specs=pl.BlockSpec((tq, DA_V_DIM), lambda t: q_block(written(t))),
        scratch_shapes=[
            pltpu.VMEM((DA_V_DIM, seq), BF16),
            pltpu.VMEM((seq, cols), F32),
            pltpu.VMEM((SUBLANES, cols), F32),
            pltpu.VMEM((DA_V_DIM, cols), F32),
            pltpu.VMEM((SUBLANES, cols), F32),
        ],
        compiler_params=_params("arbitrary"),
        name="diff_attention",
    )(qkv, qkv, qkv, lq1, lk1, lq2, lk2, subln_g.reshape(DA_V_DIM, 1))


def _fnet_in_kernel(x_ref, g_ref, w_ref, dft_ref, perm_ref, o_ref, *, radix, per_residue):
    xn = _rms(x_ref[...], g_ref[...], NORM_EPS).astype(BF16)
    xn = _dot(perm_ref[...], xn).astype(BF16)
    u = _dot(xn, w_ref[...]).astype(BF16)
    dft = dft_ref[...]
    for grp in range(FNET_GROUPS):
        lo = grp * FNET_GROUP_DIM
        ab = _dot(u[:, lo:lo + FNET_GROUP_DIM], dft).astype(o_ref.dtype)
        for n1 in range(radix):
            rows = slice(n1 * per_residue, (n1 + 1) * per_residue)
            o_ref[n1, :, lo:lo + FNET_GROUP_DIM] = ab[rows, :FNET_GROUP_DIM]
            o_ref[n1, :, D_MODEL + lo:D_MODEL + lo + FNET_GROUP_DIM] = ab[rows, FNET_GROUP_DIM:]


def _fnet_in(x, g, w, dft, batch, seq, tm):
    radix = seq // FNET_SEQ_BASE
    per_residue = tm // radix
    tiles_per_seq = seq // tm
    src = jnp.arange(tm, dtype=jnp.int32)
    dst = (src % radix) * per_residue + src // radix
    perm = (src[:, None] == dst[None, :]).astype(BF16)
    return pl.pallas_call(
        functools.partial(_fnet_in_kernel, radix=radix, per_residue=per_residue),
        out_shape=jax.ShapeDtypeStruct((batch * radix, FNET_SEQ_BASE, 2 * D_MODEL), BF16),
        grid=(batch * tiles_per_seq,),
        in_specs=[
            pl.BlockSpec((tm, D_MODEL), lambda i: (i, 0)),
            _resident((1, D_MODEL)),
            _resident((D_MODEL, D_MODEL)),
            _resident((FNET_GROUP_DIM, 2 * FNET_GROUP_DIM)),
            _resident((tm, tm)),
        ],
        out_specs=pl.BlockSpec((radix, per_residue, 2 * D_MODEL),
                               lambda i: (i // tiles_per_seq, i % tiles_per_seq, 0)),
        compiler_params=_params("parallel"),
        name="fnet_in",
    )(x, g, w, dft, perm)


def _seq_dft_kernel(a_ref, b_ref, stage1_ref, stage2_ref, o_ref, y_ref, *, radix, scale):
    base = FNET_SEQ_BASE
    for n1 in range(radix):
        m = stage1_ref[n1]
        y = (_dot(m[:, :base], a_ref[n1]) + _dot(m[:, base:], b_ref[n1])).astype(BF16)
        y_ref[0, n1] = y[:base]
        y_ref[1, n1] = y[base:]
    kb = base // radix
    g = stage2_ref[...]
    cols = y_ref.shape[-1]
    for k2 in range(0, base, kb):
        yr = y_ref[0, :, k2:k2 + kb, :].reshape(base, cols)
        yi = y_ref[1, :, k2:k2 + kb, :].reshape(base, cols)
        out = (_dot(g[:, :base], yr) + _dot(g[:, base:], yi)) * scale
        o_ref[:, k2:k2 + kb, :] = out.reshape(radix, kb, cols).astype(o_ref.dtype)


def _seq_dft(ab, stage1, stage2, batch, seq, cols):
    radix = seq // FNET_SEQ_BASE
    col_blocks = D_MODEL // cols
    scale = 1.0 / math.sqrt(seq * FNET_GROUP_DIM)
    out = pl.pallas_call(
        functools.partial(_seq_dft_kernel, radix=radix, scale=scale),
        out_shape=jax.ShapeDtypeStruct((batch * radix, FNET_SEQ_BASE, D_MODEL), BF16),
        grid=(batch, col_blocks),
        in_specs=[
            pl.BlockSpec((radix, FNET_SEQ_BASE, cols), lambda b, c: (b, 0, c)),
            pl.BlockSpec((radix, FNET_SEQ_BASE, cols), lambda b, c: (b, 0, col_blocks + c)),
            _resident((radix, 2 * FNET_SEQ_BASE, 2 * FNET_SEQ_BASE)),
            _resident((FNET_SEQ_BASE, 2 * FNET_SEQ_BASE)),
        ],
        out_specs=pl.BlockSpec((radix, FNET_SEQ_BASE, cols), lambda b, c: (b, 0, c)),
        scratch_shapes=[pltpu.VMEM((2, radix, FNET_SEQ_BASE, cols), BF16)],
        compiler_params=_params("parallel", "parallel"),
        name="seq_dft",
    )(ab, ab, stage1, stage2)
    return out.reshape(batch * seq, D_MODEL)


def _mem_kv_kernel(m_ref, g_ref, w_ref, o_ref):
    mn = _rms(m_ref[...], g_ref[...], NORM_EPS).astype(BF16)
    o_ref[...] = _dot(mn, w_ref[...]).astype(o_ref.dtype)


def _mem_kv(mem, g, w, tm):
    rows = mem.shape[0]
    return pl.pallas_call(
        _mem_kv_kernel,
        out_shape=jax.ShapeDtypeStruct((rows, 2 * D_MODEL), BF16),
        grid=(rows // tm,),
        in_specs=[
            pl.BlockSpec((tm, D_MODEL), lambda i: (i, 0)),
            _resident((1, D_MODEL)),
            _resident((D_MODEL, 2 * D_MODEL)),
        ],
        out_specs=pl.BlockSpec((tm, 2 * D_MODEL), lambda i: (i, 0)),
        compiler_params=_params("parallel"),
        name="mem_kv",
    )(mem, g, w)


def _post_mixer_kernel(x_ref, a_ref, wm_ref, g_ref, wq_ref, kv_ref, wo_ref, o_ref, heads_ref):
    x1 = x_ref[...] + _dot(a_ref[...], wm_ref[...])
    xn = _rms(x1, g_ref[...], NORM_EPS).astype(BF16)
    q = _dot(xn, wq_ref[...]).astype(BF16)
    for h in range(XA_HEADS):
        lo = h * XA_HEAD_DIM
        kh = kv_ref[:, lo:lo + XA_HEAD_DIM]
        vh = kv_ref[:, D_MODEL + lo:D_MODEL + lo + XA_HEAD_DIM]
        s = lax.dot_general(q[:, lo:lo + XA_HEAD_DIM], kh, (((1,), (1,)), ((), ())),
                            preferred_element_type=F32)
        p = jnp.exp(s - jnp.max(s, axis=-1, keepdims=True))
        denom = jnp.sum(p, axis=-1, keepdims=True)
        heads_ref[:, lo:lo + XA_HEAD_DIM] = (_dot(p.astype(BF16), vh) / denom).astype(BF16)
    o_ref[...] = x1 + _dot(heads_ref[...], wo_ref[...])


def _post_mixer(x, a, w_mix_out, g, w_q, kv, w_o, seq, mem_tokens, tm):
    tokens = x.shape[0]
    per_seq = seq // tm
    square = _resident((D_MODEL, D_MODEL))
    return pl.pallas_call(
        _post_mixer_kernel,
        out_shape=jax.ShapeDtypeStruct((tokens, D_MODEL), F32),
        grid=(tokens // tm,),
        in_specs=[
            pl.BlockSpec((tm, D_MODEL), lambda i: (i, 0)),
            pl.BlockSpec((tm, D_MODEL), lambda i: (i, 0)),
            square,
            _resident((1, D_MODEL)),
            square,
            pl.BlockSpec((mem_tokens, 2 * D_MODEL), lambda i: (i // per_seq, 0)),
            square,
        ],
        out_specs=pl.BlockSpec((tm, D_MODEL), lambda i: (i, 0)),
        scratch_shapes=[pltpu.VMEM((tm, D_MODEL), BF16)],
        compiler_params=_params("parallel"),
        name="post_mixer_xattn",
    )(x, a, w_mix_out, g, w_q, kv, w_o)


def _gelu_tanh(x):
    return x * (0.5 * (1.0 + jnp.tanh(math.sqrt(2.0 / math.pi) * (x + 0.044715 * (x * x * x)))))


def _ffn_kernel(x_ref, prev_ref, next_ref, g_ref, wup_ref, cw_ref, cb_ref, wdn_ref, gf_ref, o_ref,
                h_ref, *, tm, per_seq, final_norm):
    i = pl.program_id(0)
    x = x_ref[...]
    halo = SUBLANES
    xcat = jnp.concatenate([prev_ref[...], x, next_ref[...]], axis=0)
    xn = _rms(xcat, g_ref[...], NORM_EPS).astype(BF16)
    xn_mid = xn[halo:halo + tm]
    row = lax.broadcasted_iota(jnp.int32, (tm, 1), 0)
    keep_prev = jnp.logical_or(row > 0, i % per_seq != 0)
    keep_next = jnp.logical_or(row < tm - 1, i % per_seq != per_seq - 1)
    rows = tm + 2 * halo
    for c0 in range(0, D_FF, FFN_CHUNK):
        gate = _dot(xn, wup_ref[:, c0:c0 + FFN_CHUNK])
        val = _dot(xn_mid, wup_ref[:, D_FF + c0:D_FF + c0 + FFN_CHUNK])
        g_prev = jnp.where(keep_prev, pltpu.roll(gate, 1, 0)[halo:halo + tm], 0.0)
        g_next = jnp.where(keep_next, pltpu.roll(gate, rows - 1, 0)[halo:halo + tm], 0.0)
        g_mid = gate[halo:halo + tm]
        cw = cw_ref[:, c0:c0 + FFN_CHUNK]
        conv = g_prev * cw[0:1] + g_mid * cw[1:2] + g_next * cw[2:3] + cb_ref[:, c0:c0 + FFN_CHUNK]
        h_ref[:, c0:c0 + FFN_CHUNK] = (_gelu_tanh(conv) * val).astype(BF16)
    y = x + _dot(h_ref[...], wdn_ref[...])
    if final_norm:
        y = _rms(y, gf_ref[...], NORM_EPS)
    o_ref[...] = y


def _ffn(x, g, w_up, conv_w, conv_b, w_down, g_final, seq, tm, final_norm):
    tokens = x.shape[0]
    per_seq = seq // tm
    groups = tm // SUBLANES
    last_group = tokens // SUBLANES - 1
    return pl.pallas_call(
        functools.partial(_ffn_kernel, tm=tm, per_seq=per_seq, final_norm=final_norm),
        out_shape=jax.ShapeDtypeStruct((tokens, D_MODEL), F32),
        grid=(tokens // tm,),
        in_specs=[
            pl.BlockSpec((tm, D_MODEL), lambda i: (i, 0)),
            pl.BlockSpec((SUBLANES, D_MODEL), lambda i: (jnp.maximum(i * groups - 1, 0), 0)),
            pl.BlockSpec((SUBLANES, D_MODEL), lambda i: (jnp.minimum((i + 1) * groups, last_group), 0)),
            _resident((1, D_MODEL)),
            _resident((D_MODEL, 2 * D_FF)),
            _resident((3, D_FF)),
            _resident((1, D_FF)),
            _resident((D_FF, D_MODEL)),
            _resident((1, D_MODEL)),
        ],
        out_specs=pl.BlockSpec((tm, D_MODEL), lambda i: (i, 0)),
        scratch_shapes=[pltpu.VMEM((tm, D_FF), BF16)],
        compiler_params=_params("parallel"),
        name="ffn_convglu",
    )(x, x, x, g, w_up, conv_w, conv_b, w_down, g_final)


def _rope_tables(seq):
    half = DA_HEAD_DIM // 2
    inv_freq = ROPE_THETA ** (-jnp.arange(0, half, dtype=F32) * 2.0 / DA_HEAD_DIM)
    ang = jnp.arange(seq, dtype=F32)[:, None] * inv_freq[None, :]
    cos = jnp.tile(jnp.cos(ang), (1, 4))
    sin = jnp.sin(ang)
    sin = jnp.concatenate([-sin, -sin, sin, sin], axis=1)
    log2e = math.log2(math.e)
    return jnp.concatenate([cos * log2e, cos], axis=1), jnp.concatenate([sin * log2e, sin], axis=1)


def _dft_tables(n):
    idx = jnp.arange(n, dtype=jnp.int32)
    ang = ((idx[:, None] * idx[None, :]) % n).astype(F32) * (2.0 * math.pi / n)
    return jnp.cos(ang), jnp.sin(ang)


def _seq_dft_tables(seq):
    base = FNET_SEQ_BASE
    radix = seq // base
    assert radix * base == seq and base % radix == 0
    k2 = jnp.arange(base, dtype=jnp.int32)[None, :, None]
    n1 = jnp.arange(radix, dtype=jnp.int32)[:, None, None]
    n2 = jnp.arange(base, dtype=jnp.int32)[None, None, :]
    ang = ((k2 * (n1 + radix * n2)) % seq).astype(F32) * (2.0 * math.pi / seq)
    c, s = jnp.cos(ang), jnp.sin(ang)
    stage1 = jnp.concatenate([jnp.concatenate([c, -s], axis=2), jnp.concatenate([-s, -c], axis=2)], axis=1)
    gc, gs = _dft_tables(radix)
    eye = jnp.eye(base // radix, dtype=F32)
    stage2 = jnp.concatenate([jnp.kron(gc, eye), jnp.kron(gs, eye)], axis=1)
    return stage1.astype(BF16), stage2.astype(BF16)


def _rotary_head_layout(w):
    d_in = w.shape[0]
    w = w.reshape(d_in, DA_HEADS, 2, 2, DA_HEAD_DIM // 2)
    return w.transpose(0, 1, 3, 2, 4).reshape(d_in, D_MODEL)


def _qkv_weight(w_qkv):
    wq, wk, wv = jnp.split(w_qkv, 3, axis=1)
    wq = _rotary_head_layout(wq) * (DA_HEAD_DIM ** -0.5)
    return jnp.concatenate([wq, _rotary_head_layout(wk), wv], axis=1).astype(BF16)


def _trunk(x, mem, p):
    batch, seq, _ = x.shape
    mem_tokens = mem.shape[1]
    tokens = batch * seq
    tm = _token_tile(seq, 512)
    x = x.reshape(tokens, D_MODEL)
    mem = mem.reshape(batch * mem_tokens, D_MODEL)

    for layer in range(DEPTH):
        j = layer // N_MIXERS
        g_mix = p["norm_mix_g"][layer][None]
        if layer % N_MIXERS == 0:
            lambda_init = 0.8 - 0.6 * math.exp(-0.3 * layer)
            cos, sin = _rope_tables(seq)
            qkv = _qkv_rope(x, g_mix, p["w_qkv"][j], cos, sin, seq, tm)
            a = _diff_attention(qkv, p["lq1"][j][None], p["lk1"][j][None], p["lq2"][j][None],
                                p["lk2"][j][None], p["subln_g"][j][None], batch, seq, lambda_init,
                                tq=_token_tile(seq, 512), tk=_token_tile(seq, 512))
            w_mix_out = p["attn_w_o"][j]
        else:
            ccos, csin = _dft_tables(FNET_GROUP_DIM)
            ab = _fnet_in(x, g_mix, p["fnet_w_in"][j], jnp.concatenate([ccos, csin], axis=1).astype(BF16),
                          batch, seq, tm)
            a = _seq_dft(ab, *_seq_dft_tables(seq), batch, seq, cols=256)
            w_mix_out = p["fnet_w_out"][j]
        kv = _mem_kv(mem, p["norm_mem_g"][layer][None], p["xattn_w_kv"][layer], _token_tile(mem_tokens, 256))
        x = _post_mixer(x, a, w_mix_out, p["norm_xattn_g"][layer][None], p["xattn_w_q"][layer], kv,
                        p["xattn_w_o"][layer], seq, mem_tokens, tm)
        x = _ffn(x, p["norm_ffn_g"][layer][None], p["ffn_w_up"][layer], p["ffn_conv_w"][layer],
                 p["ffn_conv_b"][layer][None], p["ffn_w_down"][layer], p["final_norm_g"][None],
                 seq, tm, final_norm=(layer == DEPTH - 1))
    return x.reshape(batch, seq, D_MODEL)


def kernel(x_prompt, x_sample, mem_prompt, mem_sample, norm_mix_g, norm_xattn_g, norm_mem_g, norm_ffn_g, final_norm_g, attn_w_qkv, attn_lambda_q1, attn_lambda_k1, attn_lambda_q2, attn_lambda_k2, attn_subln_g, attn_w_o, fnet_w_in, fnet_w_out, xattn_w_q, xattn_w_kv, xattn_w_o, ffn_w_up, ffn_conv_w, ffn_conv_b, ffn_w_down):
    p = {
        "norm_mix_g": norm_mix_g, "norm_xattn_g": norm_xattn_g, "norm_mem_g": norm_mem_g,
        "norm_ffn_g": norm_ffn_g, "final_norm_g": final_norm_g,
        "w_qkv": jnp.stack([_qkv_weight(w) for w in attn_w_qkv]),
        "lq1": attn_lambda_q1, "lk1": attn_lambda_k1, "lq2": attn_lambda_q2, "lk2": attn_lambda_k2,
        "subln_g": attn_subln_g,
        "attn_w_o": attn_w_o.astype(BF16),
        "fnet_w_in": fnet_w_in.astype(BF16), "fnet_w_out": fnet_w_out.astype(BF16),
        "xattn_w_q": (xattn_w_q * (XA_HEAD_DIM ** -0.5)).astype(BF16),
        "xattn_w_kv": xattn_w_kv.astype(BF16), "xattn_w_o": xattn_w_o.astype(BF16),
        "ffn_w_up": ffn_w_up.astype(BF16), "ffn_conv_w": ffn_conv_w, "ffn_conv_b": ffn_conv_b,
        "ffn_w_down": ffn_w_down.astype(BF16),
    }
    return (_trunk(x_prompt, mem_prompt, p), _trunk(x_sample, mem_sample, p))
```

```python
import functools
import math

import jax
import jax.numpy as jnp
from jax import lax
from jax.experimental import pallas as pl
from jax.experimental.pallas import tpu as pltpu

D_MODEL = 1024
DEPTH = 2
N_MIXERS = 2
DA_HEADS = 8
DA_HEAD_DIM = 64
DA_V_DIM = 2 * DA_HEAD_DIM
ROPE_THETA = 10000.0
FNET_GROUPS = 4
FNET_GROUP_DIM = D_MODEL // FNET_GROUPS
FNET_SEQ_BASE = 256
XA_HEADS = 4
XA_HEAD_DIM = D_MODEL // XA_HEADS
D_FF = 2816
NORM_EPS = 1e-6
SUBLN_EPS = 1e-5

LANES = 128
SUBLANES = 8
VMEM_LIMIT_BYTES = 56 * 1024 * 1024
FFN_CHUNK = 256

F32 = jnp.float32
BF16 = jnp.bfloat16


def _params(*semantics):
    return pltpu.CompilerParams(dimension_semantics=semantics, vmem_limit_bytes=VMEM_LIMIT_BYTES)


def _resident(shape):
    return pl.BlockSpec(shape, lambda *_: (0,) * len(shape), pipeline_mode=pl.Buffered(1))


def _rms(x, g, eps):
    return x * lax.rsqrt(jnp.mean(x * x, axis=-1, keepdims=True) + eps) * g


def _dot(a, b):
    return jnp.dot(a, b, preferred_element_type=F32)


def _token_tile(seq, want):
    tile = min(seq, want)
    assert seq % tile == 0 and tile % SUBLANES == 0
    return tile


def _qkv_kernel(x_ref, g_ref, w_ref, cos_ref, sin_ref, o_ref, *, chunk):
    xn = _rms(x_ref[...], g_ref[...], NORM_EPS).astype(BF16)
    for c0 in range(0, 3 * D_MODEL, chunk):
        y = _dot(xn, w_ref[:, c0:c0 + chunk])
        for j in range(0, chunk, LANES):
            blk = y[:, j:j + LANES]
            part = (c0 + j) // D_MODEL
            if part < 2:
                cos = cos_ref[:, part * LANES:(part + 1) * LANES]
                sin = sin_ref[:, part * LANES:(part + 1) * LANES]
                blk = blk * cos + pltpu.roll(blk, LANES // 2, 1) * sin
            o_ref[:, c0 + j:c0 + j + LANES] = blk.astype(o_ref.dtype)


def _qkv_rope(x, g, w, cos, sin, seq, tm):
    tokens = x.shape[0]
    n_pos = seq // tm
    return pl.pallas_call(
        functools.partial(_qkv_kernel, chunk=512),
        out_shape=jax.ShapeDtypeStruct((tokens, 3 * D_MODEL), BF16),
        grid=(tokens // tm,),
        in_specs=[
            pl.BlockSpec((tm, D_MODEL), lambda i: (i, 0)),
            _resident((1, D_MODEL)),
            _resident((D_MODEL, 3 * D_MODEL)),
            pl.BlockSpec((tm, 2 * LANES), lambda i: (i % n_pos, 0)),
            pl.BlockSpec((tm, 2 * LANES), lambda i: (i % n_pos, 0)),
        ],
        out_specs=pl.BlockSpec((tm, 3 * D_MODEL), lambda i: (i, 0)),
        compiler_params=_params("parallel"),
        name="qkv_rope",
    )(x, g, w, cos, sin)


def _diff_attn_kernel(q_ref, k_ref, v_ref, lq1_ref, lk1_ref, lq2_ref, lk2_ref, g_ref, o_ref,
                      vt_ref, s_ref, m_ref, acc_ref, l_ref, *, tq, tk, seq, n_q, lambda_init):
    t = pl.program_id(0)
    cols = 2 * tq
    sub = SUBLANES

    @pl.when(t == 0)
    def _():
        s_ref[...] = jnp.zeros(s_ref.shape, F32)
        m_ref[...] = jnp.zeros(m_ref.shape, F32)
        acc_ref[...] = jnp.zeros(acc_ref.shape, F32)
        l_ref[...] = jnp.ones(l_ref.shape, F32)

    @pl.when(jnp.maximum(t - 1, 0) % n_q == 0)
    def _():
        vt_ref[...] = v_ref[...].astype(F32).T.astype(BF16)

    q = q_ref[...]
    lane = lax.broadcasted_iota(jnp.int32, q.shape, 1)
    first = (lane // (DA_HEAD_DIM // 2)) % 2 == 0
    zero = jnp.zeros_like(q)
    qs = jnp.concatenate([jnp.where(first, q, zero), jnp.where(first, zero, q)], axis=0)

    lam = (jnp.exp(jnp.sum(lq1_ref[...] * lk1_ref[...], axis=-1, keepdims=True))
           - jnp.exp(jnp.sum(lq2_ref[...] * lk2_ref[...], axis=-1, keepdims=True))
           + lambda_init)
    o = acc_ref[...] / jnp.sum(l_ref[...], axis=0, keepdims=True)
    o = o[:, :tq] - lam * o[:, tq:]
    o = o * lax.rsqrt(jnp.mean(o * o, axis=0, keepdims=True) + SUBLN_EPS) * g_ref[...] * (1.0 - lambda_init)
    o_ref[...] = o.T.astype(o_ref.dtype)

    m_old = m_ref[...]
    mx = lsum = acc = None
    for off in range(0, seq, tk):
        s_new = lax.dot_general(k_ref[off:off + tk, :], qs, (((1,), (1,)), ((), ())),
                                preferred_element_type=F32)
        s_old = s_ref[off:off + tk, :]
        chunk_max = jnp.max(s_new.reshape(tk // sub, sub, cols), axis=0)
        mx = chunk_max if mx is None else jnp.maximum(mx, chunk_max)
        p = jnp.exp2(s_old.reshape(tk // sub, sub, cols) - m_old)
        chunk_sum = jnp.sum(p, axis=0)
        lsum = chunk_sum if lsum is None else lsum + chunk_sum
        s_ref[off:off + tk, :] = s_new
        pv = _dot(vt_ref[:, off:off + tk], p.reshape(tk, cols).astype(BF16))
        acc = pv if acc is None else acc + pv

    acc_ref[...] = acc
    l_ref[...] = lsum
    m_ref[...] = jnp.broadcast_to(jnp.max(mx, axis=0, keepdims=True), m_ref.shape)


def _diff_attention(qkv, lq1, lk1, lq2, lk2, subln_g, batch, seq, lambda_init, tq, tk):
    n_q = seq // tq
    n_tiles = batch * DA_HEADS * n_q

    def stage_tile(t, lag):
        return jnp.clip(t - lag, 0, n_tiles - 1)

    def scored(t):
        return stage_tile(t, 0)

    def applied(t):
        return stage_tile(t, 1)

    def written(t):
        return stage_tile(t, 2)

    def head_of(tile):
        return tile // (DA_HEADS * n_q), (tile // n_q) % DA_HEADS

    def q_block(tile):
        b, h = head_of(tile)
        return b * n_q + tile % n_q, h

    def k_block(t):
        b, h = head_of(scored(t))
        return b, DA_HEADS + h

    def v_block(t):
        b, h = head_of(applied(t))
        return b, 2 * DA_HEADS + h

    vec = _resident((1, DA_HEAD_DIM))
    cols = 2 * tq
    return pl.pallas_call(
        functools.partial(_diff_attn_kernel, tq=tq, tk=tk, seq=seq, n_q=n_q, lambda_init=lambda_init),
        out_shape=jax.ShapeDtypeStruct((batch * seq, D_MODEL), BF16),
        grid=(n_tiles + 2,),
        in_specs=[
            pl.BlockSpec((tq, DA_V_DIM), lambda t: q_block(scored(t))),
            pl.BlockSpec((seq, DA_V_DIM), k_block),
            pl.BlockSpec((seq, DA_V_DIM), v_block),
            vec, vec, vec, vec,
            _resident((DA_V_DIM, 1)),
        ],
        out_specs=pl.BlockSpec((tq, DA_V_DIM), lambda t: q_block(written(t))),
        scratch_shapes=[
            pltpu.VMEM((DA_V_DIM, seq), BF16),
            pltpu.VMEM((seq, cols), F32),
            pltpu.VMEM((SUBLANES, cols), F32),
            pltpu.VMEM((DA_V_DIM, cols), F32),
            pltpu.VMEM((SUBLANES, cols), F32),
        ],
        compiler_params=_params("arbitrary"),
        name="diff_attention",
    )(qkv, qkv, qkv, lq1, lk1, lq2, lk2, subln_g.reshape(DA_V_DIM, 1))


def _fnet_in_kernel(x_ref, g_ref, w_ref, dft_ref, perm_ref, o_ref, *, radix, per_residue):
    xn = _rms(x_ref[...], g_ref[...], NORM_EPS).astype(BF16)
    xn = _dot(perm_ref[...], xn).astype(BF16)
    u = _dot(xn, w_ref[...]).astype(BF16)
    dft = dft_ref[...]
    for grp in range(FNET_GROUPS):
        lo = grp * FNET_GROUP_DIM
        ab = _dot(u[:, lo:lo + FNET_GROUP_DIM], dft).astype(o_ref.dtype)
        for n1 in range(radix):
            rows = slice(n1 * per_residue, (n1 + 1) * per_residue)
            o_ref[n1, :, lo:lo + FNET_GROUP_DIM] = ab[rows, :FNET_GROUP_DIM]
            o_ref[n1, :, D_MODEL + lo:D_MODEL + lo + FNET_GROUP_DIM] = ab[rows, FNET_GROUP_DIM:]


def _fnet_in(x, g, w, dft, batch, seq, tm):
    radix = seq // FNET_SEQ_BASE
    per_residue = tm // radix
    tiles_per_seq = seq // tm
    src = jnp.arange(tm, dtype=jnp.int32)
    dst = (src % radix) * per_residue + src // radix
    perm = (src[:, None] == dst[None, :]).astype(BF16)
    return pl.pallas_call(
        functools.partial(_fnet_in_kernel, radix=radix, per_residue=per_residue),
        out_shape=jax.ShapeDtypeStruct((batch * radix, FNET_SEQ_BASE, 2 * D_MODEL), BF16),
        grid=(batch * tiles_per_seq,),
        in_specs=[
            pl.BlockSpec((tm, D_MODEL), lambda i: (i, 0)),
            _resident((1, D_MODEL)),
            _resident((D_MODEL, D_MODEL)),
            _resident((FNET_GROUP_DIM, 2 * FNET_GROUP_DIM)),
            _resident((tm, tm)),
        ],
        out_specs=pl.BlockSpec((radix, per_residue, 2 * D_MODEL),
                               lambda i: (i // tiles_per_seq, i % tiles_per_seq, 0)),
        compiler_params=_params("parallel"),
        name="fnet_in",
    )(x, g, w, dft, perm)


def _seq_dft_kernel(a_ref, b_ref, stage1_ref, stage2_ref, o_ref, y_ref, *, radix, scale):
    base = FNET_SEQ_BASE
    for n1 in range(radix):
        m = stage1_ref[n1]
        y = (_dot(m[:, :base], a_ref[n1]) + _dot(m[:, base:], b_ref[n1])).astype(BF16)
        y_ref[0, n1] = y[:base]
        y_ref[1, n1] = y[base:]
    kb = base // radix
    g = stage2_ref[...]
    cols = y_ref.shape[-1]
    for k2 in range(0, base, kb):
        yr = y_ref[0, :, k2:k2 + kb, :].reshape(base, cols)
        yi = y_ref[1, :, k2:k2 + kb, :].reshape(base, cols)
        out = (_dot(g[:, :base], yr) + _dot(g[:, base:], yi)) * scale
        o_ref[:, k2:k2 + kb, :] = out.reshape(radix, kb, cols).astype(o_ref.dtype)


def _seq_dft(ab, stage1, stage2, batch, seq, cols):
    radix = seq // FNET_SEQ_BASE
    col_blocks = D_MODEL // cols
    scale = 1.0 / math.sqrt(seq * FNET_GROUP_DIM)
    out = pl.pallas_call(
        functools.partial(_seq_dft_kernel, radix=radix, scale=scale),
        out_shape=jax.ShapeDtypeStruct((batch * radix, FNET_SEQ_BASE, D_MODEL), BF16),
        grid=(batch, col_blocks),
        in_specs=[
            pl.BlockSpec((radix, FNET_SEQ_BASE, cols), lambda b, c: (b, 0, c)),
            pl.BlockSpec((radix, FNET_SEQ_BASE, cols), lambda b, c: (b, 0, col_blocks + c)),
            _resident((radix, 2 * FNET_SEQ_BASE, 2 * FNET_SEQ_BASE)),
            _resident((FNET_SEQ_BASE, 2 * FNET_SEQ_BASE)),
        ],
        out_specs=pl.BlockSpec((radix, FNET_SEQ_BASE, cols), lambda b, c: (b, 0, c)),
        scratch_shapes=[pltpu.VMEM((2, radix, FNET_SEQ_BASE, cols), BF16)],
        compiler_params=_params("parallel", "parallel"),
        name="seq_dft",
    )(ab, ab, stage1, stage2)
    return out.reshape(batch * seq, D_MODEL)


def _mem_kv_kernel(m_ref, g_ref, w_ref, o_ref):
    mn = _rms(m_ref[...], g_ref[...], NORM_EPS).astype(BF16)
    o_ref[...] = _dot(mn, w_ref[...]).astype(o_ref.dtype)


def _mem_kv(mem, g, w, tm):
    rows = mem.shape[0]
    return pl.pallas_call(
        _mem_kv_kernel,
        out_shape=jax.ShapeDtypeStruct((rows, 2 * D_MODEL), BF16),
        grid=(rows // tm,),
        in_specs=[
            pl.BlockSpec((tm, D_MODEL), lambda i: (i, 0)),
            _resident((1, D_MODEL)),
            _resident((D_MODEL, 2 * D_MODEL)),
        ],
        out_specs=pl.BlockSpec((tm, 2 * D_MODEL), lambda i: (i, 0)),
        compiler_params=_params("parallel"),
        name="mem_kv",
    )(mem, g, w)


def _post_mixer_kernel(x_ref, a_ref, wm_ref, g_ref, wq_ref, kv_ref, wo_ref, o_ref, heads_ref):
    x1 = x_ref[...] + _dot(a_ref[...], wm_ref[...])
    xn = _rms(x1, g_ref[...], NORM_EPS).astype(BF16)
    q = _dot(xn, wq_ref[...]).astype(BF16)
    for h in range(XA_HEADS):
        lo = h * XA_HEAD_DIM
        kh = kv_ref[:, lo:lo + XA_HEAD_DIM]
        vh = kv_ref[:, D_MODEL + lo:D_MODEL + lo + XA_HEAD_DIM]
        s = lax.dot_general(q[:, lo:lo + XA_HEAD_DIM], kh, (((1,), (1,)), ((), ())),
                            preferred_element_type=F32)
        p = jnp.exp(s - jnp.max(s, axis=-1, keepdims=True))
        denom = jnp.sum(p, axis=-1, keepdims=True)
        heads_ref[:, lo:lo + XA_HEAD_DIM] = (_dot(p.astype(BF16), vh) / denom).astype(BF16)
    o_ref[...] = x1 + _dot(heads_ref[...], wo_ref[...])


def _post_mixer(x, a, w_mix_out, g, w_q, kv, w_o, seq, mem_tokens, tm):
    tokens = x.shape[0]
    per_seq = seq // tm
    square = _resident((D_MODEL, D_MODEL))
    return pl.pallas_call(
        _post_mixer_kernel,
        out_shape=jax.ShapeDtypeStruct((tokens, D_MODEL), F32),
        grid=(tokens // tm,),
        in_specs=[
            pl.BlockSpec((tm, D_MODEL), lambda i: (i, 0)),
            pl.BlockSpec((tm, D_MODEL), lambda i: (i, 0)),
            square,
            _resident((1, D_MODEL)),
            square,
            pl.BlockSpec((mem_tokens, 2 * D_MODEL), lambda i: (i // per_seq, 0)),
            square,
        ],
        out_specs=pl.BlockSpec((tm, D_MODEL), lambda i: (i, 0)),
        scratch_shapes=[pltpu.VMEM((tm, D_MODEL), BF16)],
        compiler_params=_params("parallel"),
        name="post_mixer_xattn",
    )(x, a, w_mix_out, g, w_q, kv, w_o)


def _gelu_tanh(x):
    return x * (0.5 * (1.0 + jnp.tanh(math.sqrt(2.0 / math.pi) * (x + 0.044715 * (x * x * x)))))


def _ffn_kernel(x_ref, prev_ref, next_ref, g_ref, wup_ref, cw_ref, cb_ref, wdn_ref, gf_ref, o_ref,
                h_ref, *, tm, per_seq, final_norm):
    i = pl.program_id(0)
    x = x_ref[...]
    halo = SUBLANES
    xcat = jnp.concatenate([prev_ref[...], x, next_ref[...]], axis=0)
    xn = _rms(xcat, g_ref[...], NORM_EPS).astype(BF16)
    xn_mid = xn[halo:halo + tm]
    row = lax.broadcasted_iota(jnp.int32, (tm, 1), 0)
    keep_prev = jnp.logical_or(row > 0, i % per_seq != 0)
    keep_next = jnp.logical_or(row < tm - 1, i % per_seq != per_seq - 1)
    rows = tm + 2 * halo
    for c0 in range(0, D_FF, FFN_CHUNK):
        gate = _dot(xn, wup_ref[:, c0:c0 + FFN_CHUNK])
        val = _dot(xn_mid, wup_ref[:, D_FF + c0:D_FF + c0 + FFN_CHUNK])
        g_prev = jnp.where(keep_prev, pltpu.roll(gate, 1, 0)[halo:halo + tm], 0.0)
        g_next = jnp.where(keep_next, pltpu.roll(gate, rows - 1, 0)[halo:halo + tm], 0.0)
        g_mid = gate[halo:halo + tm]
        cw = cw_ref[:, c0:c0 + FFN_CHUNK]
        conv = g_prev * cw[0:1] + g_mid * cw[1:2] + g_next * cw[2:3] + cb_ref[:, c0:c0 + FFN_CHUNK]
        h_ref[:, c0:c0 + FFN_CHUNK] = (_gelu_tanh(conv) * val).astype(BF16)
    y = x + _dot(h_ref[...], wdn_ref[...])
    if final_norm:
        y = _rms(y, gf_ref[...], NORM_EPS)
    o_ref[...] = y


def _ffn(x, g, w_up, conv_w, conv_b, w_down, g_final, seq, tm, final_norm):
    tokens = x.shape[0]
    per_seq = seq // tm
    groups = tm // SUBLANES
    last_group = tokens // SUBLANES - 1
    return pl.pallas_call(
        functools.partial(_ffn_kernel, tm=tm, per_seq=per_seq, final_norm=final_norm),
        out_shape=jax.ShapeDtypeStruct((tokens, D_MODEL), F32),
        grid=(tokens // tm,),
        in_specs=[
            pl.BlockSpec((tm, D_MODEL), lambda i: (i, 0)),
            pl.BlockSpec((SUBLANES, D_MODEL), lambda i: (jnp.maximum(i * groups - 1, 0), 0)),
            pl.BlockSpec((SUBLANES, D_MODEL), lambda i: (jnp.minimum((i + 1) * groups, last_group), 0)),
            _resident((1, D_MODEL)),
            _resident((D_MODEL, 2 * D_FF)),
            _resident((3, D_FF)),
            _resident((1, D_FF)),
            _resident((D_FF, D_MODEL)),
            _resident((1, D_MODEL)),
        ],
        out_specs=pl.BlockSpec((tm, D_MODEL), lambda i: (i, 0)),
        scratch_shapes=[pltpu.VMEM((tm, D_FF), BF16)],
        compiler_params=_params("parallel"),
        name="ffn_convglu",
    )(x, x, x, g, w_up, conv_w, conv_b, w_down, g_final)


def _rope_tables(seq):
    half = DA_HEAD_DIM // 2
    inv_freq = ROPE_THETA ** (-jnp.arange(0, half, dtype=F32) * 2.0 / DA_HEAD_DIM)
    ang = jnp.arange(seq, dtype=F32)[:, None] * inv_freq[None, :]
    cos = jnp.tile(jnp.cos(ang), (1, 4))
    sin = jnp.sin(ang)
    sin = jnp.concatenate([-sin, -sin, sin, sin], axis=1)
    log2e = math.log2(math.e)
    return jnp.concatenate([cos * log2e, cos], axis=1), jnp.concatenate([sin * log2e, sin], axis=1)


def _dft_tables(n):
    idx = jnp.arange(n, dtype=jnp.int32)
    ang = ((idx[:, None] * idx[None, :]) % n).astype(F32) * (2.0 * math.pi / n)
    return jnp.cos(ang), jnp.sin(ang)


def _seq_dft_tables(seq):
    base = FNET_SEQ_BASE
    radix = seq // base
    assert radix * base == seq and base % radix == 0
    k2 = jnp.arange(base, dtype=jnp.int32)[None, :, None]
    n1 = jnp.arange(radix, dtype=jnp.int32)[:, None, None]
    n2 = jnp.arange(base, dtype=jnp.int32)[None, None, :]
    ang = ((k2 * (n1 + radix * n2)) % seq).astype(F32) * (2.0 * math.pi / seq)
    c, s = jnp.cos(ang), jnp.sin(ang)
    stage1 = jnp.concatenate([jnp.concatenate([c, -s], axis=2), jnp.concatenate([-s, -c], axis=2)], axis=1)
    gc, gs = _dft_tables(radix)
    eye = jnp.eye(base // radix, dtype=F32)
    stage2 = jnp.concatenate([jnp.kron(gc, eye), jnp.kron(gs, eye)], axis=1)
    return stage1.astype(BF16), stage2.astype(BF16)


def _rotary_head_layout(w):
    d_in = w.shape[0]
    w = w.reshape(d_in, DA_HEADS, 2, 2, DA_HEAD_DIM // 2)
    return w.transpose(0, 1, 3, 2, 4).reshape(d_in, D_MODEL)


def _qkv_weight(w_qkv):
    wq, wk, wv = jnp.split(w_qkv, 3, axis=1)
    wq = _rotary_head_layout(wq) * (DA_HEAD_DIM ** -0.5)
    return jnp.concatenate([wq, _rotary_head_layout(wk), wv], axis=1).astype(BF16)


def _trunk(x, mem, p):
    batch, seq, _ = x.shape
    mem_tokens = mem.shape[1]
    tokens = batch * seq
    tm = _token_tile(seq, 512)
    tm_wide = _token_tile(seq, 1024)
    x = x.reshape(tokens, D_MODEL)
    mem = mem.reshape(batch * mem_tokens, D_MODEL)

    for layer in range(DEPTH):
        j = layer // N_MIXERS
        g_mix = p["norm_mix_g"][layer][None]
        if layer % N_MIXERS == 0:
            lambda_init = 0.8 - 0.6 * math.exp(-0.3 * layer)
            cos, sin = _rope_tables(seq)
            qkv = _qkv_rope(x, g_mix, p["w_qkv"][j], cos, sin, seq, tm_wide)
            a = _diff_attention(qkv, p["lq1"][j][None], p["lk1"][j][None], p["lq2"][j][None],
                                p["lk2"][j][None], p["subln_g"][j][None], batch, seq, lambda_init,
                                tq=_token_tile(seq, 512), tk=_token_tile(seq, 512))
            w_mix_out = p["attn_w_o"][j]
        else:
            ccos, csin = _dft_tables(FNET_GROUP_DIM)
            ab = _fnet_in(x, g_mix, p["fnet_w_in"][j], jnp.concatenate([ccos, csin], axis=1).astype(BF16),
                          batch, seq, tm)
            a = _seq_dft(ab, *_seq_dft_tables(seq), batch, seq, cols=256)
            w_mix_out = p["fnet_w_out"][j]
        kv = _mem_kv(mem, p["norm_mem_g"][layer][None], p["xattn_w_kv"][layer], _token_tile(mem_tokens, 256))
        x = _post_mixer(x, a, w_mix_out, p["norm_xattn_g"][layer][None], p["xattn_w_q"][layer], kv,
                        p["xattn_w_o"][layer], seq, mem_tokens, tm_wide)
        x = _ffn(x, p["norm_ffn_g"][layer][None], p["ffn_w_up"][layer], p["ffn_conv_w"][layer],
                 p["ffn_conv_b"][layer][None], p["ffn_w_down"][layer], p["final_norm_g"][None],
                 seq, tm_wide, final_norm=(layer == DEPTH - 1))
    return x.reshape(batch, seq, D_MODEL)


def kernel(x_prompt, x_sample, mem_prompt, mem_sample, norm_mix_g, norm_xattn_g, norm_mem_g, norm_ffn_g, final_norm_g, attn_w_qkv, attn_lambda_q1, attn_lambda_k1, attn_lambda_q2, attn_lambda_k2, attn_subln_g, attn_w_o, fnet_w_in, fnet_w_out, xattn_w_q, xattn_w_kv, xattn_w_o, ffn_w_up, ffn_conv_w, ffn_conv_b, ffn_w_down):
    p = {
        "norm_mix_g": norm_mix_g, "norm_xattn_g": norm_xattn_g, "norm_mem_g": norm_mem_g,
        "norm_ffn_g": norm_ffn_g, "final_norm_g": final_norm_g,
        "w_qkv": jnp.stack([_qkv_weight(w) for w in attn_w_qkv]),
        "lq1": attn_lambda_q1, "lk1": attn_lambda_k1, "lq2": attn_lambda_q2, "lk2": attn_lambda_k2,
        "subln_g": attn_subln_g,
        "attn_w_o": attn_w_o.astype(BF16),
        "fnet_w_in": fnet_w_in.astype(BF16), "fnet_w_out": fnet_w_out.astype(BF16),
        "xattn_w_q": (xattn_w_q * (XA_HEAD_DIM ** -0.5)).astype(BF16),
        "xattn_w_kv": xattn_w_kv.astype(BF16), "xattn_w_o": xattn_w_o.astype(BF16),
        "ffn_w_up": ffn_w_up.astype(BF16), "ffn_conv_w": ffn_conv_w, "ffn_conv_b": ffn_conv_b,
        "ffn_w_down": ffn_w_down.astype(BF16),
    }
    return (_trunk(x_prompt, mem_prompt, p), _trunk(x_sample, mem_sample, p))
```

```python
import functools
import math

import jax
import jax.numpy as jnp
from jax import lax
from jax.experimental import pallas as pl
from jax.experimental.pallas import tpu as pltpu

D_MODEL = 1024
DEPTH = 2
N_MIXERS = 2
DA_HEADS = 8
DA_HEAD_DIM = 64
DA_V_DIM = 2 * DA_HEAD_DIM
ROPE_THETA = 10000.0
FNET_GROUPS = 4
FNET_GROUP_DIM = D_MODEL // FNET_GROUPS
FNET_SEQ_BASE = 256
XA_HEADS = 4
XA_HEAD_DIM = D_MODEL // XA_HEADS
D_FF = 2816
NORM_EPS = 1e-6
SUBLN_EPS = 1e-5

LANES = 128
SUBLANES = 8
VMEM_LIMIT_BYTES = 56 * 1024 * 1024
FFN_CHUNK = 256

F32 = jnp.float32
BF16 = jnp.bfloat16


def _params(*semantics):
    return pltpu.CompilerParams(dimension_semantics=semantics, vmem_limit_bytes=VMEM_LIMIT_BYTES)


def _resident(shape):
    return pl.BlockSpec(shape, lambda *_: (0,) * len(shape), pipeline_mode=pl.Buffered(1))


def _rms(x, g, eps):
    return x * lax.rsqrt(jnp.mean(x * x, axis=-1, keepdims=True) + eps) * g


def _dot(a, b):
    return jnp.dot(a, b, preferred_element_type=F32)


def _token_tile(seq, want):
    tile = min(seq, want)
    assert seq % tile == 0 and tile % SUBLANES == 0
    return tile


def _qkv_kernel(x_ref, g_ref, w_ref, cos_ref, sin_ref, o_ref, *, chunk):
    xn = _rms(x_ref[...], g_ref[...], NORM_EPS).astype(BF16)
    for c0 in range(0, 3 * D_MODEL, chunk):
        y = _dot(xn, w_ref[:, c0:c0 + chunk])
        for j in range(0, chunk, LANES):
            blk = y[:, j:j + LANES]
            part = (c0 + j) // D_MODEL
            if part < 2:
                cos = cos_ref[:, part * LANES:(part + 1) * LANES]
                sin = sin_ref[:, part * LANES:(part + 1) * LANES]
                blk = blk * cos + pltpu.roll(blk, LANES // 2, 1) * sin
            o_ref[:, c0 + j:c0 + j + LANES] = blk.astype(o_ref.dtype)


def _qkv_rope(x, g, w, cos, sin, seq, tm):
    tokens = x.shape[0]
    n_pos = seq // tm
    return pl.pallas_call(
        functools.partial(_qkv_kernel, chunk=512),
        out_shape=jax.ShapeDtypeStruct((tokens, 3 * D_MODEL), BF16),
        grid=(tokens // tm,),
        in_specs=[
            pl.BlockSpec((tm, D_MODEL), lambda i: (i, 0)),
            _resident((1, D_MODEL)),
            _resident((D_MODEL, 3 * D_MODEL)),
            pl.BlockSpec((tm, 2 * LANES), lambda i: (i % n_pos, 0)),
            pl.BlockSpec((tm, 2 * LANES), lambda i: (i % n_pos, 0)),
        ],
        out_specs=pl.BlockSpec((tm, 3 * D_MODEL), lambda i: (i, 0)),
        compiler_params=_params("parallel"),
        name="qkv_rope",
    )(x, g, w, cos, sin)


def _diff_attn_kernel(q_ref, k_ref, v_ref, lq1_ref, lk1_ref, lq2_ref, lk2_ref, g_ref, o_ref,
                      vt_ref, s_ref, m_ref, acc_ref, l_ref, *, tq, tk, seq, n_q, parts, lambda_init):
    t = pl.program_id(0)
    cols = 2 * tq
    sub = SUBLANES

    @pl.when(t == 0)
    def _():
        s_ref[...] = jnp.zeros(s_ref.shape, F32)
        m_ref[...] = jnp.zeros(m_ref.shape, F32)
        acc_ref[...] = jnp.zeros(acc_ref.shape, F32)
        l_ref[...] = jnp.ones(l_ref.shape, F32)

    @pl.when(jnp.maximum(t - 1, 0) % n_q == 0)
    def _():
        vt_ref[...] = v_ref[...].astype(F32).T.astype(BF16)

    lam = (jnp.exp(jnp.sum(lq1_ref[...] * lk1_ref[...], axis=-1, keepdims=True))
           - jnp.exp(jnp.sum(lq2_ref[...] * lk2_ref[...], axis=-1, keepdims=True))
           + lambda_init)

    for part in range(parts):
        rows = slice(part * tq, (part + 1) * tq)
        span = slice(part * cols, (part + 1) * cols)

        q = q_ref[rows, :]
        lane = lax.broadcasted_iota(jnp.int32, q.shape, 1)
        first = (lane // (DA_HEAD_DIM // 2)) % 2 == 0
        zero = jnp.zeros_like(q)
        qs = jnp.concatenate([jnp.where(first, q, zero), jnp.where(first, zero, q)], axis=0)

        o = acc_ref[:, span] / jnp.sum(l_ref[:, span], axis=0, keepdims=True)
        o = o[:, :tq] - lam * o[:, tq:]
        o = o * lax.rsqrt(jnp.mean(o * o, axis=0, keepdims=True) + SUBLN_EPS) * g_ref[...] * (1.0 - lambda_init)
        o_ref[rows, :] = o.T.astype(o_ref.dtype)

        m_old = m_ref[:, span]
        mx = lsum = acc = None
        for off in range(0, seq, tk):
            s_new = lax.dot_general(k_ref[off:off + tk, :], qs, (((1,), (1,)), ((), ())),
                                    preferred_element_type=F32)
            s_old = s_ref[off:off + tk, span]
            chunk_max = jnp.max(s_new.reshape(tk // sub, sub, cols), axis=0)
            mx = chunk_max if mx is None else jnp.maximum(mx, chunk_max)
            p = jnp.exp2(s_old.reshape(tk // sub, sub, cols) - m_old)
            chunk_sum = jnp.sum(p, axis=0)
            lsum = chunk_sum if lsum is None else lsum + chunk_sum
            s_ref[off:off + tk, span] = s_new
            pv = _dot(vt_ref[:, off:off + tk], p.reshape(tk, cols).astype(BF16))
            acc = pv if acc is None else acc + pv

        acc_ref[:, span] = acc
        l_ref[:, span] = lsum
        m_ref[:, span] = jnp.broadcast_to(jnp.max(mx, axis=0, keepdims=True), (sub, cols))


def _diff_attention(qkv, lq1, lk1, lq2, lk2, subln_g, batch, seq, lambda_init, tq, tk, parts):
    block = parts * tq
    assert seq % block == 0
    n_q = seq // block
    n_tiles = batch * DA_HEADS * n_q

    def stage_tile(t, lag):
        return jnp.clip(t - lag, 0, n_tiles - 1)

    def scored(t):
        return stage_tile(t, 0)

    def applied(t):
        return stage_tile(t, 1)

    def written(t):
        return stage_tile(t, 2)

    def head_of(tile):
        return tile // (DA_HEADS * n_q), (tile // n_q) % DA_HEADS

    def q_block(tile):
        b, h = head_of(tile)
        return b * n_q + tile % n_q, h

    def k_block(t):
        b, h = head_of(scored(t))
        return b, DA_HEADS + h

    def v_block(t):
        b, h = head_of(applied(t))
        return b, 2 * DA_HEADS + h

    vec = _resident((1, DA_HEAD_DIM))
    cols = 2 * block
    return pl.pallas_call(
        functools.partial(_diff_attn_kernel, tq=tq, tk=tk, seq=seq, n_q=n_q, parts=parts,
                          lambda_init=lambda_init),
        out_shape=jax.ShapeDtypeStruct((batch * seq, D_MODEL), BF16),
        grid=(n_tiles + 2,),
        in_specs=[
            pl.BlockSpec((block, DA_V_DIM), lambda t: q_block(scored(t))),
            pl.BlockSpec((seq, DA_V_DIM), k_block),
            pl.BlockSpec((seq, DA_V_DIM), v_block),
            vec, vec, vec, vec,
            _resident((DA_V_DIM, 1)),
        ],
        out_specs=pl.BlockSpec((block, DA_V_DIM), lambda t: q_block(written(t))),
        scratch_shapes=[
            pltpu.VMEM((DA_V_DIM, seq), BF16),
            pltpu.VMEM((seq, cols), F32),
            pltpu.VMEM((SUBLANES, cols), F32),
            pltpu.VMEM((DA_V_DIM, cols), F32),
            pltpu.VMEM((SUBLANES, cols), F32),
        ],
        compiler_params=_params("arbitrary"),
        name="diff_attention",
    )(qkv, qkv, qkv, lq1, lk1, lq2, lk2, subln_g.reshape(DA_V_DIM, 1))


def _fnet_in_kernel(x_ref, g_ref, w_ref, dft_ref, perm_ref, o_ref, *, radix, per_residue):
    xn = _rms(x_ref[...], g_ref[...], NORM_EPS).astype(BF16)
    xn = _dot(perm_ref[...], xn).astype(BF16)
    u = _dot(xn, w_ref[...]).astype(BF16)
    dft = dft_ref[...]
    for grp in range(FNET_GROUPS):
        lo = grp * FNET_GROUP_DIM
        ab = _dot(u[:, lo:lo + FNET_GROUP_DIM], dft).astype(o_ref.dtype)
        for n1 in range(radix):
            rows = slice(n1 * per_residue, (n1 + 1) * per_residue)
            o_ref[n1, :, lo:lo + FNET_GROUP_DIM] = ab[rows, :FNET_GROUP_DIM]
            o_ref[n1, :, D_MODEL + lo:D_MODEL + lo + FNET_GROUP_DIM] = ab[rows, FNET_GROUP_DIM:]


def _fnet_in(x, g, w, dft, batch, seq, tm):
    radix = seq // FNET_SEQ_BASE
    per_residue = tm // radix
    tiles_per_seq = seq // tm
    src = jnp.arange(tm, dtype=jnp.int32)
    dst = (src % radix) * per_residue + src // radix
    perm = (src[:, None] == dst[None, :]).astype(BF16)
    return pl.pallas_call(
        functools.partial(_fnet_in_kernel, radix=radix, per_residue=per_residue),
        out_shape=jax.ShapeDtypeStruct((batch * radix, FNET_SEQ_BASE, 2 * D_MODEL), BF16),
        grid=(batch * tiles_per_seq,),
        in_specs=[
            pl.BlockSpec((tm, D_MODEL), lambda i: (i, 0)),
            _resident((1, D_MODEL)),
            _resident((D_MODEL, D_MODEL)),
            _resident((FNET_GROUP_DIM, 2 * FNET_GROUP_DIM)),
            _resident((tm, tm)),
        ],
        out_specs=pl.BlockSpec((radix, per_residue, 2 * D_MODEL),
                               lambda i: (i // tiles_per_seq, i % tiles_per_seq, 0)),
        compiler_params=_params("parallel"),
        name="fnet_in",
    )(x, g, w, dft, perm)


def _seq_dft_kernel(a_ref, b_ref, stage1_ref, stage2_ref, o_ref, y_ref, *, radix, scale):
    base = FNET_SEQ_BASE
    for n1 in range(radix):
        m = stage1_ref[n1]
        y = (_dot(m[:, :base], a_ref[n1]) + _dot(m[:, base:], b_ref[n1])).astype(BF16)
        y_ref[0, n1] = y[:base]
        y_ref[1, n1] = y[base:]
    kb = base // radix
    g = stage2_ref[...]
    cols = y_ref.shape[-1]
    for k2 in range(0, base, kb):
        yr = y_ref[0, :, k2:k2 + kb, :].reshape(base, cols)
        yi = y_ref[1, :, k2:k2 + kb, :].reshape(base, cols)
        out = (_dot(g[:, :base], yr) + _dot(g[:, base:], yi)) * scale
        o_ref[:, k2:k2 + kb, :] = out.reshape(radix, kb, cols).astype(o_ref.dtype)


def _seq_dft(ab, stage1, stage2, batch, seq, cols):
    radix = seq // FNET_SEQ_BASE
    col_blocks = D_MODEL // cols
    scale = 1.0 / math.sqrt(seq * FNET_GROUP_DIM)
    out = pl.pallas_call(
        functools.partial(_seq_dft_kernel, radix=radix, scale=scale),
        out_shape=jax.ShapeDtypeStruct((batch * radix, FNET_SEQ_BASE, D_MODEL), BF16),
        grid=(batch, col_blocks),
        in_specs=[
            pl.BlockSpec((radix, FNET_SEQ_BASE, cols), lambda b, c: (b, 0, c)),
            pl.BlockSpec((radix, FNET_SEQ_BASE, cols), lambda b, c: (b, 0, col_blocks + c)),
            _resident((radix, 2 * FNET_SEQ_BASE, 2 * FNET_SEQ_BASE)),
            _resident((FNET_SEQ_BASE, 2 * FNET_SEQ_BASE)),
        ],
        out_specs=pl.BlockSpec((radix, FNET_SEQ_BASE, cols), lambda b, c: (b, 0, c)),
        scratch_shapes=[pltpu.VMEM((2, radix, FNET_SEQ_BASE, cols), BF16)],
        compiler_params=_params("parallel", "parallel"),
        name="seq_dft",
    )(ab, ab, stage1, stage2)
    return out.reshape(batch * seq, D_MODEL)


def _mem_kv_kernel(m_ref, g_ref, w_ref, o_ref):
    mn = _rms(m_ref[...], g_ref[...], NORM_EPS).astype(BF16)
    o_ref[...] = _dot(mn, w_ref[...]).astype(o_ref.dtype)


def _mem_kv(mem, g, w, tm):
    rows = mem.shape[0]
    return pl.pallas_call(
        _mem_kv_kernel,
        out_shape=jax.ShapeDtypeStruct((rows, 2 * D_MODEL), BF16),
        grid=(rows // tm,),
        in_specs=[
            pl.BlockSpec((tm, D_MODEL), lambda i: (i, 0)),
            _resident((1, D_MODEL)),
            _resident((D_MODEL, 2 * D_MODEL)),
        ],
        out_specs=pl.BlockSpec((tm, 2 * D_MODEL), lambda i: (i, 0)),
        compiler_params=_params("parallel"),
        name="mem_kv",
    )(mem, g, w)


def _post_mixer_kernel(x_ref, a_ref, wm_ref, g_ref, wq_ref, kv_ref, wo_ref, o_ref, heads_ref):
    x1 = x_ref[...] + _dot(a_ref[...], wm_ref[...])
    xn = _rms(x1, g_ref[...], NORM_EPS).astype(BF16)
    q = _dot(xn, wq_ref[...]).astype(BF16)
    for h in range(XA_HEADS):
        lo = h * XA_HEAD_DIM
        kh = kv_ref[:, lo:lo + XA_HEAD_DIM]
        vh = kv_ref[:, D_MODEL + lo:D_MODEL + lo + XA_HEAD_DIM]
        s = lax.dot_general(q[:, lo:lo + XA_HEAD_DIM], kh, (((1,), (1,)), ((), ())),
                            preferred_element_type=F32)
        p = jnp.exp(s - jnp.max(s, axis=-1, keepdims=True))
        denom = jnp.sum(p, axis=-1, keepdims=True)
        heads_ref[:, lo:lo + XA_HEAD_DIM] = (_dot(p.astype(BF16), vh) / denom).astype(BF16)
    o_ref[...] = x1 + _dot(heads_ref[...], wo_ref[...])


def _post_mixer(x, a, w_mix_out, g, w_q, kv, w_o, seq, mem_tokens, tm):
    tokens = x.shape[0]
    per_seq = seq // tm
    square = _resident((D_MODEL, D_MODEL))
    return pl.pallas_call(
        _post_mixer_kernel,
        out_shape=jax.ShapeDtypeStruct((tokens, D_MODEL), F32),
        grid=(tokens // tm,),
        in_specs=[
            pl.BlockSpec((tm, D_MODEL), lambda i: (i, 0)),
            pl.BlockSpec((tm, D_MODEL), lambda i: (i, 0)),
            square,
            _resident((1, D_MODEL)),
            square,
            pl.BlockSpec((mem_tokens, 2 * D_MODEL), lambda i: (i // per_seq, 0)),
            square,
        ],
        out_specs=pl.BlockSpec((tm, D_MODEL), lambda i: (i, 0)),
        scratch_shapes=[pltpu.VMEM((tm, D_MODEL), BF16)],
        compiler_params=_params("parallel"),
        name="post_mixer_xattn",
    )(x, a, w_mix_out, g, w_q, kv, w_o)


def _gelu_tanh(x):
    return x * (0.5 * (1.0 + jnp.tanh(math.sqrt(2.0 / math.pi) * (x + 0.044715 * (x * x * x)))))


def _ffn_kernel(x_ref, prev_ref, next_ref, g_ref, wup_ref, cw_ref, cb_ref, wdn_ref, gf_ref, o_ref,
                h_ref, *, tm, per_seq, final_norm):
    i = pl.program_id(0)
    x = x_ref[...]
    halo = SUBLANES
    xcat = jnp.concatenate([prev_ref[...], x, next_ref[...]], axis=0)
    xn = _rms(xcat, g_ref[...], NORM_EPS).astype(BF16)
    xn_mid = xn[halo:halo + tm]
    row = lax.broadcasted_iota(jnp.int32, (tm, 1), 0)
    keep_prev = jnp.logical_or(row > 0, i % per_seq != 0)
    keep_next = jnp.logical_or(row < tm - 1, i % per_seq != per_seq - 1)
    rows = tm + 2 * halo
    for c0 in range(0, D_FF, FFN_CHUNK):
        gate = _dot(xn, wup_ref[:, c0:c0 + FFN_CHUNK])
        val = _dot(xn_mid, wup_ref[:, D_FF + c0:D_FF + c0 + FFN_CHUNK])
        g_prev = jnp.where(keep_prev, pltpu.roll(gate, 1, 0)[halo:halo + tm], 0.0)
        g_next = jnp.where(keep_next, pltpu.roll(gate, rows - 1, 0)[halo:halo + tm], 0.0)
        g_mid = gate[halo:halo + tm]
        cw = cw_ref[:, c0:c0 + FFN_CHUNK]
        conv = g_prev * cw[0:1] + g_mid * cw[1:2] + g_next * cw[2:3] + cb_ref[:, c0:c0 + FFN_CHUNK]
        h_ref[:, c0:c0 + FFN_CHUNK] = (_gelu_tanh(conv) * val).astype(BF16)
    y = x + _dot(h_ref[...], wdn_ref[...])
    if final_norm:
        y = _rms(y, gf_ref[...], NORM_EPS)
    o_ref[...] = y


def _ffn(x, g, w_up, conv_w, conv_b, w_down, g_final, seq, tm, final_norm):
    tokens = x.shape[0]
    per_seq = seq // tm
    groups = tm // SUBLANES
    last_group = tokens // SUBLANES - 1
    return pl.pallas_call(
        functools.partial(_ffn_kernel, tm=tm, per_seq=per_seq, final_norm=final_norm),
        out_shape=jax.ShapeDtypeStruct((tokens, D_MODEL), F32),
        grid=(tokens // tm,),
        in_specs=[
            pl.BlockSpec((tm, D_MODEL), lambda i: (i, 0)),
            pl.BlockSpec((SUBLANES, D_MODEL), lambda i: (jnp.maximum(i * groups - 1, 0), 0)),
            pl.BlockSpec((SUBLANES, D_MODEL), lambda i: (jnp.minimum((i + 1) * groups, last_group), 0)),
            _resident((1, D_MODEL)),
            _resident((D_MODEL, 2 * D_FF)),
            _resident((3, D_FF)),
            _resident((1, D_FF)),
            _resident((D_FF, D_MODEL)),
            _resident((1, D_MODEL)),
        ],
        out_specs=pl.BlockSpec((tm, D_MODEL), lambda i: (i, 0)),
        scratch_shapes=[pltpu.VMEM((tm, D_FF), BF16)],
        compiler_params=_params("parallel"),
        name="ffn_convglu",
    )(x, x, x, g, w_up, conv_w, conv_b, w_down, g_final)


def _rope_tables(seq):
    half = DA_HEAD_DIM // 2
    inv_freq = ROPE_THETA ** (-jnp.arange(0, half, dtype=F32) * 2.0 / DA_HEAD_DIM)
    ang = jnp.arange(seq, dtype=F32)[:, None] * inv_freq[None, :]
    cos = jnp.tile(jnp.cos(ang), (1, 4))
    sin = jnp.sin(ang)
    sin = jnp.concatenate([-sin, -sin, sin, sin], axis=1)
    log2e = math.log2(math.e)
    return jnp.concatenate([cos * log2e, cos], axis=1), jnp.concatenate([sin * log2e, sin], axis=1)


def _dft_tables(n):
    idx = jnp.arange(n, dtype=jnp.int32)
    ang = ((idx[:, None] * idx[None, :]) % n).astype(F32) * (2.0 * math.pi / n)
    return jnp.cos(ang), jnp.sin(ang)


def _seq_dft_tables(seq):
    base = FNET_SEQ_BASE
    radix = seq // base
    assert radix * base == seq and base % radix == 0
    k2 = jnp.arange(base, dtype=jnp.int32)[None, :, None]
    n1 = jnp.arange(radix, dtype=jnp.int32)[:, None, None]
    n2 = jnp.arange(base, dtype=jnp.int32)[None, None, :]
    ang = ((k2 * (n1 + radix * n2)) % seq).astype(F32) * (2.0 * math.pi / seq)
    c, s = jnp.cos(ang), jnp.sin(ang)
    stage1 = jnp.concatenate([jnp.concatenate([c, -s], axis=2), jnp.concatenate([-s, -c], axis=2)], axis=1)
    gc, gs = _dft_tables(radix)
    eye = jnp.eye(base // radix, dtype=F32)
    stage2 = jnp.concatenate([jnp.kron(gc, eye), jnp.kron(gs, eye)], axis=1)
    return stage1.astype(BF16), stage2.astype(BF16)


def _rotary_head_layout(w):
    d_in = w.shape[0]
    w = w.reshape(d_in, DA_HEADS, 2, 2, DA_HEAD_DIM // 2)
    return w.transpose(0, 1, 3, 2, 4).reshape(d_in, D_MODEL)


def _qkv_weight(w_qkv):
    wq, wk, wv = jnp.split(w_qkv, 3, axis=1)
    wq = _rotary_head_layout(wq) * (DA_HEAD_DIM ** -0.5)
    return jnp.concatenate([wq, _rotary_head_layout(wk), wv], axis=1).astype(BF16)


def _trunk(x, mem, p):
    batch, seq, _ = x.shape
    mem_tokens = mem.shape[1]
    tokens = batch * seq
    tm = _token_tile(seq, 512)
    tm_wide = _token_tile(seq, 1024)
    x = x.reshape(tokens, D_MODEL)
    mem = mem.reshape(batch * mem_tokens, D_MODEL)

    for layer in range(DEPTH):
        j = layer // N_MIXERS
        g_mix = p["norm_mix_g"][layer][None]
        if layer % N_MIXERS == 0:
            lambda_init = 0.8 - 0.6 * math.exp(-0.3 * layer)
            cos, sin = _rope_tables(seq)
            qkv = _qkv_rope(x, g_mix, p["w_qkv"][j], cos, sin, seq, tm_wide)
            a = _diff_attention(qkv, p["lq1"][j][None], p["lk1"][j][None], p["lq2"][j][None],
                                p["lk2"][j][None], p["subln_g"][j][None], batch, seq, lambda_init,
                                tq=_token_tile(seq, 512), tk=_token_tile(seq, 512),
                                parts=2 if seq % 1024 == 0 else 1)
            w_mix_out = p["attn_w_o"][j]
        else:
            ccos, csin = _dft_tables(FNET_GROUP_DIM)
            ab = _fnet_in(x, g_mix, p["fnet_w_in"][j], jnp.concatenate([ccos, csin], axis=1).astype(BF16),
                          batch, seq, tm)
            a = _seq_dft(ab, *_seq_dft_tables(seq), batch, seq, cols=256)
            w_mix_out = p["fnet_w_out"][j]
        kv = _mem_kv(mem, p["norm_mem_g"][layer][None], p["xattn_w_kv"][layer], _token_tile(mem_tokens, 256))
        x = _post_mixer(x, a, w_mix_out, p["norm_xattn_g"][layer][None], p["xattn_w_q"][layer], kv,
                        p["xattn_w_o"][layer], seq, mem_tokens, tm_wide)
        x = _ffn(x, p["norm_ffn_g"][layer][None], p["ffn_w_up"][layer], p["ffn_conv_w"][layer],
                 p["ffn_conv_b"][layer][None], p["ffn_w_down"][layer], p["final_norm_g"][None],
                 seq, tm_wide, final_norm=(layer == DEPTH - 1))
    return x.reshape(batch, seq, D_MODEL)


def kernel(x_prompt, x_sample, mem_prompt, mem_sample, norm_mix_g, norm_xattn_g, norm_mem_g, norm_ffn_g, final_norm_g, attn_w_qkv, attn_lambda_q1, attn_lambda_k1, attn_lambda_q2, attn_lambda_k2, attn_subln_g, attn_w_o, fnet_w_in, fnet_w_out, xattn_w_q, xattn_w_kv, xattn_w_o, ffn_w_up, ffn_conv_w, ffn_conv_b, ffn_w_down):
    p = {
        "norm_mix_g": norm_mix_g, "norm_xattn_g": norm_xattn_g, "norm_mem_g": norm_mem_g,
        "norm_ffn_g": norm_ffn_g, "final_norm_g": final_norm_g,
        "w_qkv": jnp.stack([_qkv_weight(w) for w in attn_w_qkv]),
        "lq1": attn_lambda_q1, "lk1": attn_lambda_k1, "lq2": attn_lambda_q2, "lk2": attn_lambda_k2,
        "subln_g": attn_subln_g,
        "attn_w_o": attn_w_o.astype(BF16),
        "fnet_w_in": fnet_w_in.astype(BF16), "fnet_w_out": fnet_w_out.astype(BF16),
        "xattn_w_q": (xattn_w_q * (XA_HEAD_DIM ** -0.5)).astype(BF16),
        "xattn_w_kv": xattn_w_kv.astype(BF16), "xattn_w_o": xattn_w_o.astype(BF16),
        "ffn_w_up": ffn_w_up.astype(BF16), "ffn_conv_w": ffn_conv_w, "ffn_conv_b": ffn_conv_b,
        "ffn_w_down": ffn_w_down.astype(BF16),
    }
    return (_trunk(x_prompt, mem_prompt, p), _trunk(x_sample, mem_sample, p))
```

```python
import functools
import math

import jax
import jax.numpy as jnp
from jax import lax
from jax.experimental import pallas as pl
from jax.experimental.pallas import tpu as pltpu

D_MODEL = 1024
DEPTH = 2
N_MIXERS = 2
DA_HEADS = 8
DA_HEAD_DIM = 64
DA_V_DIM = 2 * DA_HEAD_DIM
ROPE_THETA = 10000.0
FNET_GROUPS = 4
FNET_GROUP_DIM = D_MODEL // FNET_GROUPS
FNET_SEQ_BASE = 256
XA_HEADS = 4
XA_HEAD_DIM = D_MODEL // XA_HEADS
D_FF = 2816
NORM_EPS = 1e-6
SUBLN_EPS = 1e-5

LANES = 128
SUBLANES = 8
VMEM_LIMIT_BYTES = 56 * 1024 * 1024
FFN_CHUNK = 256

F32 = jnp.float32
BF16 = jnp.bfloat16


def _params(*semantics):
    return pltpu.CompilerParams(dimension_semantics=semantics, vmem_limit_bytes=VMEM_LIMIT_BYTES)


def _resident(shape):
    return pl.BlockSpec(shape, lambda *_: (0,) * len(shape), pipeline_mode=pl.Buffered(1))


def _rms(x, g, eps):
    return x * lax.rsqrt(jnp.mean(x * x, axis=-1, keepdims=True) + eps) * g


def _dot(a, b):
    return jnp.dot(a, b, preferred_element_type=F32)


def _token_tile(seq, want):
    tile = min(seq, want)
    assert seq % tile == 0 and tile % SUBLANES == 0
    return tile


def _qkv_kernel(x_ref, g_ref, w_ref, cos_ref, sin_ref, o_ref, *, chunk):
    xn = _rms(x_ref[...], g_ref[...], NORM_EPS).astype(BF16)
    for c0 in range(0, 3 * D_MODEL, chunk):
        y = _dot(xn, w_ref[:, c0:c0 + chunk])
        for j in range(0, chunk, LANES):
            blk = y[:, j:j + LANES]
            part = (c0 + j) // D_MODEL
            if part < 2:
                cos = cos_ref[:, part * LANES:(part + 1) * LANES]
                sin = sin_ref[:, part * LANES:(part + 1) * LANES]
                blk = blk * cos + pltpu.roll(blk, LANES // 2, 1) * sin
            o_ref[:, c0 + j:c0 + j + LANES] = blk.astype(o_ref.dtype)


def _qkv_rope(x, g, w, cos, sin, seq, tm):
    tokens = x.shape[0]
    n_pos = seq // tm
    return pl.pallas_call(
        functools.partial(_qkv_kernel, chunk=512),
        out_shape=jax.ShapeDtypeStruct((tokens, 3 * D_MODEL), BF16),
        grid=(tokens // tm,),
        in_specs=[
            pl.BlockSpec((tm, D_MODEL), lambda i: (i, 0)),
            _resident((1, D_MODEL)),
            _resident((D_MODEL, 3 * D_MODEL)),
            pl.BlockSpec((tm, 2 * LANES), lambda i: (i % n_pos, 0)),
            pl.BlockSpec((tm, 2 * LANES), lambda i: (i % n_pos, 0)),
        ],
        out_specs=pl.BlockSpec((tm, 3 * D_MODEL), lambda i: (i, 0)),
        compiler_params=_params("parallel"),
        name="qkv_rope",
    )(x, g, w, cos, sin)


def _diff_attn_kernel(q_ref, k_ref, v_ref, lq1_ref, lk1_ref, lq2_ref, lk2_ref, g_ref, o_ref,
                      vt_ref, s_ref, m_ref, acc_ref, l_ref, *, tq, tk, seq, n_q, parts, lambda_init):
    t = pl.program_id(0)
    cols = 2 * tq
    sub = SUBLANES

    @pl.when(t == 0)
    def _():
        s_ref[...] = jnp.zeros(s_ref.shape, F32)
        m_ref[...] = jnp.zeros(m_ref.shape, F32)
        acc_ref[...] = jnp.zeros(acc_ref.shape, F32)
        l_ref[...] = jnp.ones(l_ref.shape, F32)

    @pl.when(jnp.maximum(t - 1, 0) % n_q == 0)
    def _():
        vt_ref[...] = v_ref[...].astype(F32).T.astype(BF16)

    lam = (jnp.exp(jnp.sum(lq1_ref[...] * lk1_ref[...], axis=-1, keepdims=True))
           - jnp.exp(jnp.sum(lq2_ref[...] * lk2_ref[...], axis=-1, keepdims=True))
           + lambda_init)

    for part in range(parts):
        rows = slice(part * tq, (part + 1) * tq)
        span = slice(part * cols, (part + 1) * cols)

        q = q_ref[rows, :]
        lane = lax.broadcasted_iota(jnp.int32, q.shape, 1)
        first = (lane // (DA_HEAD_DIM // 2)) % 2 == 0
        zero = jnp.zeros_like(q)
        qs = jnp.concatenate([jnp.where(first, q, zero), jnp.where(first, zero, q)], axis=0)

        o = acc_ref[:, span] / jnp.sum(l_ref[:, span], axis=0, keepdims=True)
        o = o[:, :tq] - lam * o[:, tq:]
        o = o * lax.rsqrt(jnp.mean(o * o, axis=0, keepdims=True) + SUBLN_EPS) * g_ref[...] * (1.0 - lambda_init)
        o_ref[rows, :] = o.T.astype(o_ref.dtype)

        m_old = m_ref[:, span]
        mx = lsum = acc = None
        for off in range(0, seq, tk):
            s_new = lax.dot_general(k_ref[off:off + tk, :], qs, (((1,), (1,)), ((), ())),
                                    preferred_element_type=F32)
            s_old = s_ref[off:off + tk, span]
            chunk_max = jnp.max(s_new.reshape(tk // sub, sub, cols), axis=0)
            mx = chunk_max if mx is None else jnp.maximum(mx, chunk_max)
            p = jnp.exp2(s_old.reshape(tk // sub, sub, cols) - m_old)
            chunk_sum = jnp.sum(p, axis=0)
            lsum = chunk_sum if lsum is None else lsum + chunk_sum
            s_ref[off:off + tk, span] = s_new
            pv = _dot(vt_ref[:, off:off + tk], p.reshape(tk, cols).astype(BF16))
            acc = pv if acc is None else acc + pv

        acc_ref[:, span] = acc
        l_ref[:, span] = lsum
        m_ref[:, span] = jnp.broadcast_to(jnp.max(mx, axis=0, keepdims=True), (sub, cols))


def _diff_attention(qkv, lq1, lk1, lq2, lk2, subln_g, batch, seq, lambda_init, tq, tk, parts):
    block = parts * tq
    assert seq % block == 0
    n_q = seq // block
    n_tiles = batch * DA_HEADS * n_q

    def stage_tile(t, lag):
        return jnp.clip(t - lag, 0, n_tiles - 1)

    def scored(t):
        return stage_tile(t, 0)

    def applied(t):
        return stage_tile(t, 1)

    def written(t):
        return stage_tile(t, 2)

    def head_of(tile):
        return tile // (DA_HEADS * n_q), (tile // n_q) % DA_HEADS

    def q_block(tile):
        b, h = head_of(tile)
        return b * n_q + tile % n_q, h

    def k_block(t):
        b, h = head_of(scored(t))
        return b, DA_HEADS + h

    def v_block(t):
        b, h = head_of(applied(t))
        return b, 2 * DA_HEADS + h

    vec = _resident((1, DA_HEAD_DIM))
    cols = 2 * block
    return pl.pallas_call(
        functools.partial(_diff_attn_kernel, tq=tq, tk=tk, seq=seq, n_q=n_q, parts=parts,
                          lambda_init=lambda_init),
        out_shape=jax.ShapeDtypeStruct((batch * seq, D_MODEL), BF16),
        grid=(n_tiles + 2,),
        in_specs=[
            pl.BlockSpec((block, DA_V_DIM), lambda t: q_block(scored(t))),
            pl.BlockSpec((seq, DA_V_DIM), k_block),
            pl.BlockSpec((seq, DA_V_DIM), v_block),
            vec, vec, vec, vec,
            _resident((DA_V_DIM, 1)),
        ],
        out_specs=pl.BlockSpec((block, DA_V_DIM), lambda t: q_block(written(t))),
        scratch_shapes=[
            pltpu.VMEM((DA_V_DIM, seq), BF16),
            pltpu.VMEM((seq, cols), F32),
            pltpu.VMEM((SUBLANES, cols), F32),
            pltpu.VMEM((DA_V_DIM, cols), F32),
            pltpu.VMEM((SUBLANES, cols), F32),
        ],
        compiler_params=_params("arbitrary"),
        name="diff_attention",
    )(qkv, qkv, qkv, lq1, lk1, lq2, lk2, subln_g.reshape(DA_V_DIM, 1))


def _fnet_in_kernel(x_ref, g_ref, w_ref, dft_ref, perm_ref, o_ref, *, radix, per_residue):
    xn = _rms(x_ref[...], g_ref[...], NORM_EPS).astype(BF16)
    xn = _dot(perm_ref[...], xn).astype(BF16)
    u = _dot(xn, w_ref[...]).astype(BF16)
    dft = dft_ref[...]
    for grp in range(FNET_GROUPS):
        lo = grp * FNET_GROUP_DIM
        ab = _dot(u[:, lo:lo + FNET_GROUP_DIM], dft).astype(o_ref.dtype)
        for n1 in range(radix):
            rows = slice(n1 * per_residue, (n1 + 1) * per_residue)
            o_ref[n1, :, lo:lo + FNET_GROUP_DIM] = ab[rows, :FNET_GROUP_DIM]
            o_ref[n1, :, D_MODEL + lo:D_MODEL + lo + FNET_GROUP_DIM] = ab[rows, FNET_GROUP_DIM:]


def _fnet_in(x, g, w, dft, batch, seq, tm):
    radix = seq // FNET_SEQ_BASE
    per_residue = tm // radix
    tiles_per_seq = seq // tm
    src = jnp.arange(tm, dtype=jnp.int32)
    dst = (src % radix) * per_residue + src // radix
    perm = (src[:, None] == dst[None, :]).astype(BF16)
    return pl.pallas_call(
        functools.partial(_fnet_in_kernel, radix=radix, per_residue=per_residue),
        out_shape=jax.ShapeDtypeStruct((batch * radix, FNET_SEQ_BASE, 2 * D_MODEL), BF16),
        grid=(batch * tiles_per_seq,),
        in_specs=[
            pl.BlockSpec((tm, D_MODEL), lambda i: (i, 0)),
            _resident((1, D_MODEL)),
            _resident((D_MODEL, D_MODEL)),
            _resident((FNET_GROUP_DIM, 2 * FNET_GROUP_DIM)),
            _resident((tm, tm)),
        ],
        out_specs=pl.BlockSpec((radix, per_residue, 2 * D_MODEL),
                               lambda i: (i // tiles_per_seq, i % tiles_per_seq, 0)),
        compiler_params=_params("parallel"),
        name="fnet_in",
    )(x, g, w, dft, perm)


def _seq_dft_kernel(a_ref, b_ref, stage1_ref, stage2_ref, o_ref, y_ref, *, radix, scale):
    base = FNET_SEQ_BASE
    for n1 in range(radix):
        m = stage1_ref[n1]
        y = (_dot(m[:, :base], a_ref[n1]) + _dot(m[:, base:], b_ref[n1])).astype(BF16)
        y_ref[0, n1] = y[:base]
        y_ref[1, n1] = y[base:]
    kb = base // radix
    g = stage2_ref[...]
    cols = y_ref.shape[-1]
    for k2 in range(0, base, kb):
        yr = y_ref[0, :, k2:k2 + kb, :].reshape(base, cols)
        yi = y_ref[1, :, k2:k2 + kb, :].reshape(base, cols)
        out = (_dot(g[:, :base], yr) + _dot(g[:, base:], yi)) * scale
        o_ref[:, k2:k2 + kb, :] = out.reshape(radix, kb, cols).astype(o_ref.dtype)


def _seq_dft(ab, stage1, stage2, batch, seq, cols):
    radix = seq // FNET_SEQ_BASE
    col_blocks = D_MODEL // cols
    scale = 1.0 / math.sqrt(seq * FNET_GROUP_DIM)
    out = pl.pallas_call(
        functools.partial(_seq_dft_kernel, radix=radix, scale=scale),
        out_shape=jax.ShapeDtypeStruct((batch * radix, FNET_SEQ_BASE, D_MODEL), BF16),
        grid=(batch, col_blocks),
        in_specs=[
            pl.BlockSpec((radix, FNET_SEQ_BASE, cols), lambda b, c: (b, 0, c)),
            pl.BlockSpec((radix, FNET_SEQ_BASE, cols), lambda b, c: (b, 0, col_blocks + c)),
            _resident((radix, 2 * FNET_SEQ_BASE, 2 * FNET_SEQ_BASE)),
            _resident((FNET_SEQ_BASE, 2 * FNET_SEQ_BASE)),
        ],
        out_specs=pl.BlockSpec((radix, FNET_SEQ_BASE, cols), lambda b, c: (b, 0, c)),
        scratch_shapes=[pltpu.VMEM((2, radix, FNET_SEQ_BASE, cols), BF16)],
        compiler_params=_params("parallel", "parallel"),
        name="seq_dft",
    )(ab, ab, stage1, stage2)
    return out.reshape(batch * seq, D_MODEL)


def _mem_kv_kernel(m_ref, g_ref, w_ref, o_ref):
    mn = _rms(m_ref[...], g_ref[...], NORM_EPS).astype(BF16)
    o_ref[...] = _dot(mn, w_ref[...]).astype(o_ref.dtype)


def _mem_kv(mem, g, w, tm):
    rows = mem.shape[0]
    return pl.pallas_call(
        _mem_kv_kernel,
        out_shape=jax.ShapeDtypeStruct((rows, 2 * D_MODEL), BF16),
        grid=(rows // tm,),
        in_specs=[
            pl.BlockSpec((tm, D_MODEL), lambda i: (i, 0)),
            _resident((1, D_MODEL)),
            _resident((D_MODEL, 2 * D_MODEL)),
        ],
        out_specs=pl.BlockSpec((tm, 2 * D_MODEL), lambda i: (i, 0)),
        compiler_params=_params("parallel"),
        name="mem_kv",
    )(mem, g, w)


def _post_mixer_kernel(x_ref, a_ref, wm_ref, g_ref, wq_ref, kv_ref, wo_ref, o_ref, heads_ref):
    x1 = x_ref[...] + _dot(a_ref[...], wm_ref[...])
    xn = _rms(x1, g_ref[...], NORM_EPS).astype(BF16)
    q = _dot(xn, wq_ref[...]).astype(BF16)
    for h in range(XA_HEADS):
        lo = h * XA_HEAD_DIM
        kh = kv_ref[:, lo:lo + XA_HEAD_DIM]
        vh = kv_ref[:, D_MODEL + lo:D_MODEL + lo + XA_HEAD_DIM]
        s = lax.dot_general(q[:, lo:lo + XA_HEAD_DIM], kh, (((1,), (1,)), ((), ())),
                            preferred_element_type=F32)
        p = jnp.exp(s - jnp.max(s, axis=-1, keepdims=True))
        denom = jnp.sum(p, axis=-1, keepdims=True)
        heads_ref[:, lo:lo + XA_HEAD_DIM] = (_dot(p.astype(BF16), vh) / denom).astype(BF16)
    o_ref[...] = x1 + _dot(heads_ref[...], wo_ref[...])


def _post_mixer(x, a, w_mix_out, g, w_q, kv, w_o, seq, mem_tokens, tm):
    tokens = x.shape[0]
    per_seq = seq // tm
    square = _resident((D_MODEL, D_MODEL))
    return pl.pallas_call(
        _post_mixer_kernel,
        out_shape=jax.ShapeDtypeStruct((tokens, D_MODEL), F32),
        grid=(tokens // tm,),
        in_specs=[
            pl.BlockSpec((tm, D_MODEL), lambda i: (i, 0)),
            pl.BlockSpec((tm, D_MODEL), lambda i: (i, 0)),
            square,
            _resident((1, D_MODEL)),
            square,
            pl.BlockSpec((mem_tokens, 2 * D_MODEL), lambda i: (i // per_seq, 0)),
            square,
        ],
        out_specs=pl.BlockSpec((tm, D_MODEL), lambda i: (i, 0)),
        scratch_shapes=[pltpu.VMEM((tm, D_MODEL), BF16)],
        compiler_params=_params("parallel"),
        name="post_mixer_xattn",
    )(x, a, w_mix_out, g, w_q, kv, w_o)


def _gelu_tanh(x):
    return x * (0.5 * (1.0 + jnp.tanh(math.sqrt(2.0 / math.pi) * (x + 0.044715 * (x * x * x)))))


def _ffn_kernel(x_ref, prev_ref, next_ref, g_ref, wup_ref, cw_ref, cb_ref, wdn_ref, gf_ref, o_ref,
                h_ref, *, tm, per_seq, final_norm):
    i = pl.program_id(0)
    x = x_ref[...]
    halo = SUBLANES
    xcat = jnp.concatenate([prev_ref[...], x, next_ref[...]], axis=0)
    xn = _rms(xcat, g_ref[...], NORM_EPS).astype(BF16)
    xn_mid = xn[halo:halo + tm]
    row = lax.broadcasted_iota(jnp.int32, (tm, 1), 0)
    keep_prev = jnp.logical_or(row > 0, i % per_seq != 0)
    keep_next = jnp.logical_or(row < tm - 1, i % per_seq != per_seq - 1)
    rows = tm + 2 * halo
    for c0 in range(0, D_FF, FFN_CHUNK):
        gate = _dot(xn, wup_ref[:, c0:c0 + FFN_CHUNK])
        val = _dot(xn_mid, wup_ref[:, D_FF + c0:D_FF + c0 + FFN_CHUNK])
        g_prev = jnp.where(keep_prev, pltpu.roll(gate, 1, 0)[halo:halo + tm], 0.0)
        g_next = jnp.where(keep_next, pltpu.roll(gate, rows - 1, 0)[halo:halo + tm], 0.0)
        g_mid = gate[halo:halo + tm]
        cw = cw_ref[:, c0:c0 + FFN_CHUNK]
        conv = g_prev * cw[0:1] + g_mid * cw[1:2] + g_next * cw[2:3] + cb_ref[:, c0:c0 + FFN_CHUNK]
        h_ref[:, c0:c0 + FFN_CHUNK] = (_gelu_tanh(conv) * val).astype(BF16)
    y = x + _dot(h_ref[...], wdn_ref[...])
    if final_norm:
        y = _rms(y, gf_ref[...], NORM_EPS)
    o_ref[...] = y


def _ffn(x, g, w_up, conv_w, conv_b, w_down, g_final, seq, tm, final_norm):
    tokens = x.shape[0]
    per_seq = seq // tm
    groups = tm // SUBLANES
    last_group = tokens // SUBLANES - 1
    return pl.pallas_call(
        functools.partial(_ffn_kernel, tm=tm, per_seq=per_seq, final_norm=final_norm),
        out_shape=jax.ShapeDtypeStruct((tokens, D_MODEL), F32),
        grid=(tokens // tm,),
        in_specs=[
            pl.BlockSpec((tm, D_MODEL), lambda i: (i, 0)),
            pl.BlockSpec((SUBLANES, D_MODEL), lambda i: (jnp.maximum(i * groups - 1, 0), 0)),
            pl.BlockSpec((SUBLANES, D_MODEL), lambda i: (jnp.minimum((i + 1) * groups, last_group), 0)),
            _resident((1, D_MODEL)),
            _resident((D_MODEL, 2 * D_FF)),
            _resident((3, D_FF)),
            _resident((1, D_FF)),
            _resident((D_FF, D_MODEL)),
            _resident((1, D_MODEL)),
        ],
        out_specs=pl.BlockSpec((tm, D_MODEL), lambda i: (i, 0)),
        scratch_shapes=[pltpu.VMEM((tm, D_FF), BF16)],
        compiler_params=_params("parallel"),
        name="ffn_convglu",
    )(x, x, x, g, w_up, conv_w, conv_b, w_down, g_final)


def _rope_tables(seq):
    half = DA_HEAD_DIM // 2
    inv_freq = ROPE_THETA ** (-jnp.arange(0, half, dtype=F32) * 2.0 / DA_HEAD_DIM)
    ang = jnp.arange(seq, dtype=F32)[:, None] * inv_freq[None, :]
    cos = jnp.tile(jnp.cos(ang), (1, 4))
    sin = jnp.sin(ang)
    sin = jnp.concatenate([-sin, -sin, sin, sin], axis=1)
    log2e = math.log2(math.e)
    return jnp.concatenate([cos * log2e, cos], axis=1), jnp.concatenate([sin * log2e, sin], axis=1)


def _dft_tables(n):
    idx = jnp.arange(n, dtype=jnp.int32)
    ang = ((idx[:, None] * idx[None, :]) % n).astype(F32) * (2.0 * math.pi / n)
    return jnp.cos(ang), jnp.sin(ang)


def _seq_dft_tables(seq):
    base = FNET_SEQ_BASE
    radix = seq // base
    assert radix * base == seq and base % radix == 0
    k2 = jnp.arange(base, dtype=jnp.int32)[None, :, None]
    n1 = jnp.arange(radix, dtype=jnp.int32)[:, None, None]
    n2 = jnp.arange(base, dtype=jnp.int32)[None, None, :]
    ang = ((k2 * (n1 + radix * n2)) % seq).astype(F32) * (2.0 * math.pi / seq)
    c, s = jnp.cos(ang), jnp.sin(ang)
    stage1 = jnp.concatenate([jnp.concatenate([c, -s], axis=2), jnp.concatenate([-s, -c], axis=2)], axis=1)
    gc, gs = _dft_tables(radix)
    eye = jnp.eye(base // radix, dtype=F32)
    stage2 = jnp.concatenate([jnp.kron(gc, eye), jnp.kron(gs, eye)], axis=1)
    return stage1.astype(BF16), stage2.astype(BF16)


def _rotary_head_layout(w):
    d_in = w.shape[0]
    w = w.reshape(d_in, DA_HEADS, 2, 2, DA_HEAD_DIM // 2)
    return w.transpose(0, 1, 3, 2, 4).reshape(d_in, D_MODEL)


def _qkv_weight(w_qkv):
    wq, wk, wv = jnp.split(w_qkv, 3, axis=1)
    wq = _rotary_head_layout(wq) * (DA_HEAD_DIM ** -0.5)
    return jnp.concatenate([wq, _rotary_head_layout(wk), wv], axis=1).astype(BF16)


def _trunk(x, mem, p):
    batch, seq, _ = x.shape
    mem_tokens = mem.shape[1]
    tokens = batch * seq
    tm = _token_tile(seq, 512)
    tm_wide = _token_tile(seq, 1024)
    x = x.reshape(tokens, D_MODEL)
    mem = mem.reshape(batch * mem_tokens, D_MODEL)

    for layer in range(DEPTH):
        j = layer // N_MIXERS
        g_mix = p["norm_mix_g"][layer][None]
        if layer % N_MIXERS == 0:
            lambda_init = 0.8 - 0.6 * math.exp(-0.3 * layer)
            cos, sin = _rope_tables(seq)
            qkv = _qkv_rope(x, g_mix, p["w_qkv"][j], cos, sin, seq, tm_wide)
            a = _diff_attention(qkv, p["lq1"][j][None], p["lk1"][j][None], p["lq2"][j][None],
                                p["lk2"][j][None], p["subln_g"][j][None], batch, seq, lambda_init,
                                tq=_token_tile(seq, 256), tk=_token_tile(seq, 256),
                                parts=4 if seq % 1024 == 0 else 1)
            w_mix_out = p["attn_w_o"][j]
        else:
            ccos, csin = _dft_tables(FNET_GROUP_DIM)
            ab = _fnet_in(x, g_mix, p["fnet_w_in"][j], jnp.concatenate([ccos, csin], axis=1).astype(BF16),
                          batch, seq, tm)
            a = _seq_dft(ab, *_seq_dft_tables(seq), batch, seq, cols=256)
            w_mix_out = p["fnet_w_out"][j]
        kv = _mem_kv(mem, p["norm_mem_g"][layer][None], p["xattn_w_kv"][layer], _token_tile(mem_tokens, 256))
        x = _post_mixer(x, a, w_mix_out, p["norm_xattn_g"][layer][None], p["xattn_w_q"][layer], kv,
                        p["xattn_w_o"][layer], seq, mem_tokens, tm_wide)
        x = _ffn(x, p["norm_ffn_g"][layer][None], p["ffn_w_up"][layer], p["ffn_conv_w"][layer],
                 p["ffn_conv_b"][layer][None], p["ffn_w_down"][layer], p["final_norm_g"][None],
                 seq, tm_wide, final_norm=(layer == DEPTH - 1))
    return x.reshape(batch, seq, D_MODEL)


def kernel(x_prompt, x_sample, mem_prompt, mem_sample, norm_mix_g, norm_xattn_g, norm_mem_g, norm_ffn_g, final_norm_g, attn_w_qkv, attn_lambda_q1, attn_lambda_k1, attn_lambda_q2, attn_lambda_k2, attn_subln_g, attn_w_o, fnet_w_in, fnet_w_out, xattn_w_q, xattn_w_kv, xattn_w_o, ffn_w_up, ffn_conv_w, ffn_conv_b, ffn_w_down):
    p = {
        "norm_mix_g": norm_mix_g, "norm_xattn_g": norm_xattn_g, "norm_mem_g": norm_mem_g,
        "norm_ffn_g": norm_ffn_g, "final_norm_g": final_norm_g,
        "w_qkv": jnp.stack([_qkv_weight(w) for w in attn_w_qkv]),
        "lq1": attn_lambda_q1, "lk1": attn_lambda_k1, "lq2": attn_lambda_q2, "lk2": attn_lambda_k2,
        "subln_g": attn_subln_g,
        "attn_w_o": attn_w_o.astype(BF16),
        "fnet_w_in": fnet_w_in.astype(BF16), "fnet_w_out": fnet_w_out.astype(BF16),
        "xattn_w_q": (xattn_w_q * (XA_HEAD_DIM ** -0.5)).astype(BF16),
        "xattn_w_kv": xattn_w_kv.astype(BF16), "xattn_w_o": xattn_w_o.astype(BF16),
        "ffn_w_up": ffn_w_up.astype(BF16), "ffn_conv_w": ffn_conv_w, "ffn_conv_b": ffn_conv_b,
        "ffn_w_down": ffn_w_down.astype(BF16),
    }
    return (_trunk(x_prompt, mem_prompt, p), _trunk(x_sample, mem_sample, p))
```

```python
import functools
import math

import jax
import jax.numpy as jnp
from jax import lax
from jax.experimental import pallas as pl
from jax.experimental.pallas import tpu as pltpu

D_MODEL = 1024
DEPTH = 2
N_MIXERS = 2
DA_HEADS = 8
DA_HEAD_DIM = 64
DA_V_DIM = 2 * DA_HEAD_DIM
ROPE_THETA = 10000.0
FNET_GROUPS = 4
FNET_GROUP_DIM = D_MODEL // FNET_GROUPS
FNET_SEQ_BASE = 256
XA_HEADS = 4
XA_HEAD_DIM = D_MODEL // XA_HEADS
D_FF = 2816
NORM_EPS = 1e-6
SUBLN_EPS = 1e-5

LANES = 128
SUBLANES = 8
VMEM_LIMIT_BYTES = 56 * 1024 * 1024
FFN_CHUNK = 256
ATTN_SCORE_SCRATCH_BYTES = 32 * 1024 * 1024

F32 = jnp.float32
BF16 = jnp.bfloat16


def _params(*semantics):
    return pltpu.CompilerParams(dimension_semantics=semantics, vmem_limit_bytes=VMEM_LIMIT_BYTES)


def _resident(shape):
    return pl.BlockSpec(shape, lambda *_: (0,) * len(shape), pipeline_mode=pl.Buffered(1))


def _rms(x, g, eps):
    return x * lax.rsqrt(jnp.mean(x * x, axis=-1, keepdims=True) + eps) * g


def _dot(a, b):
    return jnp.dot(a, b, preferred_element_type=F32)


def _token_tile(seq, want):
    tile = min(seq, want)
    assert seq % tile == 0 and tile % SUBLANES == 0
    return tile


def _qkv_kernel(x_ref, g_ref, w_ref, cos_ref, sin_ref, o_ref, *, chunk):
    xn = _rms(x_ref[...], g_ref[...], NORM_EPS).astype(BF16)
    for c0 in range(0, 3 * D_MODEL, chunk):
        y = _dot(xn, w_ref[:, c0:c0 + chunk])
        for j in range(0, chunk, LANES):
            blk = y[:, j:j + LANES]
            part = (c0 + j) // D_MODEL
            if part < 2:
                cos = cos_ref[:, part * LANES:(part + 1) * LANES]
                sin = sin_ref[:, part * LANES:(part + 1) * LANES]
                blk = blk * cos + pltpu.roll(blk, LANES // 2, 1) * sin
            o_ref[:, c0 + j:c0 + j + LANES] = blk.astype(o_ref.dtype)


def _qkv_rope(x, g, w, cos, sin, seq, tm):
    tokens = x.shape[0]
    n_pos = seq // tm
    return pl.pallas_call(
        functools.partial(_qkv_kernel, chunk=512),
        out_shape=jax.ShapeDtypeStruct((tokens, 3 * D_MODEL), BF16),
        grid=(tokens // tm,),
        in_specs=[
            pl.BlockSpec((tm, D_MODEL), lambda i: (i, 0)),
            _resident((1, D_MODEL)),
            _resident((D_MODEL, 3 * D_MODEL)),
            pl.BlockSpec((tm, 2 * LANES), lambda i: (i % n_pos, 0)),
            pl.BlockSpec((tm, 2 * LANES), lambda i: (i % n_pos, 0)),
        ],
        out_specs=pl.BlockSpec((tm, 3 * D_MODEL), lambda i: (i, 0)),
        compiler_params=_params("parallel"),
        name="qkv_rope",
    )(x, g, w, cos, sin)


def _diff_attn_kernel(q_ref, k_ref, v_ref, lq1_ref, lk1_ref, lq2_ref, lk2_ref, g_ref, o_ref,
                      vt_ref, s_ref, m_ref, acc_ref, l_ref, *, tq, tk, seq, n_q, parts, lambda_init):
    t = pl.program_id(0)
    cols = 2 * tq
    sub = SUBLANES

    @pl.when(t == 0)
    def _():
        s_ref[...] = jnp.zeros(s_ref.shape, F32)
        m_ref[...] = jnp.zeros(m_ref.shape, F32)
        acc_ref[...] = jnp.zeros(acc_ref.shape, F32)
        l_ref[...] = jnp.ones(l_ref.shape, F32)

    @pl.when(jnp.maximum(t - 1, 0) % n_q == 0)
    def _():
        vt_ref[...] = v_ref[...].astype(F32).T.astype(BF16)

    lam = (jnp.exp(jnp.sum(lq1_ref[...] * lk1_ref[...], axis=-1, keepdims=True))
           - jnp.exp(jnp.sum(lq2_ref[...] * lk2_ref[...], axis=-1, keepdims=True))
           + lambda_init)

    for part in range(parts):
        rows = slice(part * tq, (part + 1) * tq)
        span = slice(part * cols, (part + 1) * cols)

        q = q_ref[rows, :]
        lane = lax.broadcasted_iota(jnp.int32, q.shape, 1)
        first = (lane // (DA_HEAD_DIM // 2)) % 2 == 0
        zero = jnp.zeros_like(q)
        qs = jnp.concatenate([jnp.where(first, q, zero), jnp.where(first, zero, q)], axis=0)

        o = acc_ref[:, span] / jnp.sum(l_ref[:, span], axis=0, keepdims=True)
        o = o[:, :tq] - lam * o[:, tq:]
        o = o * lax.rsqrt(jnp.mean(o * o, axis=0, keepdims=True) + SUBLN_EPS) * g_ref[...] * (1.0 - lambda_init)
        o_ref[rows, :] = o.T.astype(o_ref.dtype)

        m_old = m_ref[:, span]
        mx = lsum = acc = None
        for off in range(0, seq, tk):
            s_new = lax.dot_general(k_ref[off:off + tk, :], qs, (((1,), (1,)), ((), ())),
                                    preferred_element_type=F32)
            s_old = s_ref[off:off + tk, span]
            chunk_max = jnp.max(s_new.reshape(tk // sub, sub, cols), axis=0)
            mx = chunk_max if mx is None else jnp.maximum(mx, chunk_max)
            p = jnp.exp2(s_old.reshape(tk // sub, sub, cols) - m_old)
            chunk_sum = jnp.sum(p, axis=0)
            lsum = chunk_sum if lsum is None else lsum + chunk_sum
            s_ref[off:off + tk, span] = s_new
            pv = _dot(vt_ref[:, off:off + tk], p.reshape(tk, cols).astype(BF16))
            acc = pv if acc is None else acc + pv

        acc_ref[:, span] = acc
        l_ref[:, span] = lsum
        m_ref[:, span] = jnp.broadcast_to(jnp.max(mx, axis=0, keepdims=True), (sub, cols))


def _diff_attention(qkv, lq1, lk1, lq2, lk2, subln_g, batch, seq, lambda_init, tq, tk, parts):
    block = parts * tq
    assert seq % block == 0
    n_q = seq // block
    n_tiles = batch * DA_HEADS * n_q

    def stage_tile(t, lag):
        return jnp.clip(t - lag, 0, n_tiles - 1)

    def scored(t):
        return stage_tile(t, 0)

    def applied(t):
        return stage_tile(t, 1)

    def written(t):
        return stage_tile(t, 2)

    def head_of(tile):
        return tile // (DA_HEADS * n_q), (tile // n_q) % DA_HEADS

    def q_block(tile):
        b, h = head_of(tile)
        return b * n_q + tile % n_q, h

    def k_block(t):
        b, h = head_of(scored(t))
        return b, DA_HEADS + h

    def v_block(t):
        b, h = head_of(applied(t))
        return b, 2 * DA_HEADS + h

    vec = _resident((1, DA_HEAD_DIM))
    cols = 2 * block
    return pl.pallas_call(
        functools.partial(_diff_attn_kernel, tq=tq, tk=tk, seq=seq, n_q=n_q, parts=parts,
                          lambda_init=lambda_init),
        out_shape=jax.ShapeDtypeStruct((batch * seq, D_MODEL), BF16),
        grid=(n_tiles + 2,),
        in_specs=[
            pl.BlockSpec((block, DA_V_DIM), lambda t: q_block(scored(t))),
            pl.BlockSpec((seq, DA_V_DIM), k_block),
            pl.BlockSpec((seq, DA_V_DIM), v_block),
            vec, vec, vec, vec,
            _resident((DA_V_DIM, 1)),
        ],
        out_specs=pl.BlockSpec((block, DA_V_DIM), lambda t: q_block(written(t))),
        scratch_shapes=[
            pltpu.VMEM((DA_V_DIM, seq), BF16),
            pltpu.VMEM((seq, cols), F32),
            pltpu.VMEM((SUBLANES, cols), F32),
            pltpu.VMEM((DA_V_DIM, cols), F32),
            pltpu.VMEM((SUBLANES, cols), F32),
        ],
        compiler_params=_params("arbitrary"),
        name="diff_attention",
    )(qkv, qkv, qkv, lq1, lk1, lq2, lk2, subln_g.reshape(DA_V_DIM, 1))


def _fnet_in_kernel(x_ref, g_ref, w_ref, dft_ref, perm_ref, o_ref, *, radix, per_residue):
    xn = _rms(x_ref[...], g_ref[...], NORM_EPS).astype(BF16)
    xn = _dot(perm_ref[...], xn).astype(BF16)
    u = _dot(xn, w_ref[...]).astype(BF16)
    dft = dft_ref[...]
    for grp in range(FNET_GROUPS):
        lo = grp * FNET_GROUP_DIM
        ab = _dot(u[:, lo:lo + FNET_GROUP_DIM], dft).astype(o_ref.dtype)
        for n1 in range(radix):
            rows = slice(n1 * per_residue, (n1 + 1) * per_residue)
            o_ref[n1, :, lo:lo + FNET_GROUP_DIM] = ab[rows, :FNET_GROUP_DIM]
            o_ref[n1, :, D_MODEL + lo:D_MODEL + lo + FNET_GROUP_DIM] = ab[rows, FNET_GROUP_DIM:]


def _fnet_in(x, g, w, dft, batch, seq, tm):
    radix = seq // FNET_SEQ_BASE
    per_residue = tm // radix
    tiles_per_seq = seq // tm
    src = jnp.arange(tm, dtype=jnp.int32)
    dst = (src % radix) * per_residue + src // radix
    perm = (src[:, None] == dst[None, :]).astype(BF16)
    return pl.pallas_call(
        functools.partial(_fnet_in_kernel, radix=radix, per_residue=per_residue),
        out_shape=jax.ShapeDtypeStruct((batch * radix, FNET_SEQ_BASE, 2 * D_MODEL), BF16),
        grid=(batch * tiles_per_seq,),
        in_specs=[
            pl.BlockSpec((tm, D_MODEL), lambda i: (i, 0)),
            _resident((1, D_MODEL)),
            _resident((D_MODEL, D_MODEL)),
            _resident((FNET_GROUP_DIM, 2 * FNET_GROUP_DIM)),
            _resident((tm, tm)),
        ],
        out_specs=pl.BlockSpec((radix, per_residue, 2 * D_MODEL),
                               lambda i: (i // tiles_per_seq, i % tiles_per_seq, 0)),
        compiler_params=_params("parallel"),
        name="fnet_in",
    )(x, g, w, dft, perm)


def _seq_dft_kernel(a_ref, b_ref, stage1_ref, stage2_ref, o_ref, y_ref, *, radix, scale):
    base = FNET_SEQ_BASE
    for n1 in range(radix):
        m = stage1_ref[n1]
        y = (_dot(m[:, :base], a_ref[n1]) + _dot(m[:, base:], b_ref[n1])).astype(BF16)
        y_ref[0, n1] = y[:base]
        y_ref[1, n1] = y[base:]
    kb = base // radix
    g = stage2_ref[...]
    cols = y_ref.shape[-1]
    for k2 in range(0, base, kb):
        yr = y_ref[0, :, k2:k2 + kb, :].reshape(base, cols)
        yi = y_ref[1, :, k2:k2 + kb, :].reshape(base, cols)
        out = (_dot(g[:, :base], yr) + _dot(g[:, base:], yi)) * scale
        o_ref[:, k2:k2 + kb, :] = out.reshape(radix, kb, cols).astype(o_ref.dtype)


def _seq_dft(ab, stage1, stage2, batch, seq, cols):
    radix = seq // FNET_SEQ_BASE
    col_blocks = D_MODEL // cols
    scale = 1.0 / math.sqrt(seq * FNET_GROUP_DIM)
    out = pl.pallas_call(
        functools.partial(_seq_dft_kernel, radix=radix, scale=scale),
        out_shape=jax.ShapeDtypeStruct((batch * radix, FNET_SEQ_BASE, D_MODEL), BF16),
        grid=(batch, col_blocks),
        in_specs=[
            pl.BlockSpec((radix, FNET_SEQ_BASE, cols), lambda b, c: (b, 0, c)),
            pl.BlockSpec((radix, FNET_SEQ_BASE, cols), lambda b, c: (b, 0, col_blocks + c)),
            _resident((radix, 2 * FNET_SEQ_BASE, 2 * FNET_SEQ_BASE)),
            _resident((FNET_SEQ_BASE, 2 * FNET_SEQ_BASE)),
        ],
        out_specs=pl.BlockSpec((radix, FNET_SEQ_BASE, cols), lambda b, c: (b, 0, c)),
        scratch_shapes=[pltpu.VMEM((2, radix, FNET_SEQ_BASE, cols), BF16)],
        compiler_params=_params("parallel", "parallel"),
        name="seq_dft",
    )(ab, ab, stage1, stage2)
    return out.reshape(batch * seq, D_MODEL)


def _mem_kv_kernel(m_ref, g_ref, w_ref, o_ref):
    mn = _rms(m_ref[...], g_ref[...], NORM_EPS).astype(BF16)
    o_ref[...] = _dot(mn, w_ref[...]).astype(o_ref.dtype)


def _mem_kv(mem, g, w, tm):
    rows = mem.shape[0]
    return pl.pallas_call(
        _mem_kv_kernel,
        out_shape=jax.ShapeDtypeStruct((rows, 2 * D_MODEL), BF16),
        grid=(rows // tm,),
        in_specs=[
            pl.BlockSpec((tm, D_MODEL), lambda i: (i, 0)),
            _resident((1, D_MODEL)),
            _resident((D_MODEL, 2 * D_MODEL)),
        ],
        out_specs=pl.BlockSpec((tm, 2 * D_MODEL), lambda i: (i, 0)),
        compiler_params=_params("parallel"),
        name="mem_kv",
    )(mem, g, w)


def _post_mixer_kernel(x_ref, a_ref, wm_ref, g_ref, wq_ref, kv_ref, wo_ref, o_ref, heads_ref):
    x1 = x_ref[...] + _dot(a_ref[...], wm_ref[...])
    xn = _rms(x1, g_ref[...], NORM_EPS).astype(BF16)
    q = _dot(xn, wq_ref[...]).astype(BF16)
    for h in range(XA_HEADS):
        lo = h * XA_HEAD_DIM
        kh = kv_ref[:, lo:lo + XA_HEAD_DIM]
        vh = kv_ref[:, D_MODEL + lo:D_MODEL + lo + XA_HEAD_DIM]
        s = lax.dot_general(q[:, lo:lo + XA_HEAD_DIM], kh, (((1,), (1,)), ((), ())),
                            preferred_element_type=F32)
        p = jnp.exp(s - jnp.max(s, axis=-1, keepdims=True))
        denom = jnp.sum(p, axis=-1, keepdims=True)
        heads_ref[:, lo:lo + XA_HEAD_DIM] = (_dot(p.astype(BF16), vh) / denom).astype(BF16)
    o_ref[...] = x1 + _dot(heads_ref[...], wo_ref[...])


def _post_mixer(x, a, w_mix_out, g, w_q, kv, w_o, seq, mem_tokens, tm):
    tokens = x.shape[0]
    per_seq = seq // tm
    square = _resident((D_MODEL, D_MODEL))
    return pl.pallas_call(
        _post_mixer_kernel,
        out_shape=jax.ShapeDtypeStruct((tokens, D_MODEL), F32),
        grid=(tokens // tm,),
        in_specs=[
            pl.BlockSpec((tm, D_MODEL), lambda i: (i, 0)),
            pl.BlockSpec((tm, D_MODEL), lambda i: (i, 0)),
            square,
            _resident((1, D_MODEL)),
            square,
            pl.BlockSpec((mem_tokens, 2 * D_MODEL), lambda i: (i // per_seq, 0)),
            square,
        ],
        out_specs=pl.BlockSpec((tm, D_MODEL), lambda i: (i, 0)),
        scratch_shapes=[pltpu.VMEM((tm, D_MODEL), BF16)],
        compiler_params=_params("parallel"),
        name="post_mixer_xattn",
    )(x, a, w_mix_out, g, w_q, kv, w_o)


def _gelu_tanh(x):
    return x * (0.5 * (1.0 + jnp.tanh(math.sqrt(2.0 / math.pi) * (x + 0.044715 * (x * x * x)))))


def _ffn_kernel(x_ref, prev_ref, next_ref, g_ref, wup_ref, cw_ref, cb_ref, wdn_ref, gf_ref, o_ref,
                h_ref, *, tm, per_seq, final_norm):
    i = pl.program_id(0)
    x = x_ref[...]
    halo = SUBLANES
    xcat = jnp.concatenate([prev_ref[...], x, next_ref[...]], axis=0)
    xn = _rms(xcat, g_ref[...], NORM_EPS).astype(BF16)
    xn_mid = xn[halo:halo + tm]
    row = lax.broadcasted_iota(jnp.int32, (tm, 1), 0)
    keep_prev = jnp.logical_or(row > 0, i % per_seq != 0)
    keep_next = jnp.logical_or(row < tm - 1, i % per_seq != per_seq - 1)
    rows = tm + 2 * halo
    for c0 in range(0, D_FF, FFN_CHUNK):
        gate = _dot(xn, wup_ref[:, c0:c0 + FFN_CHUNK])
        val = _dot(xn_mid, wup_ref[:, D_FF + c0:D_FF + c0 + FFN_CHUNK])
        g_prev = jnp.where(keep_prev, pltpu.roll(gate, 1, 0)[halo:halo + tm], 0.0)
        g_next = jnp.where(keep_next, pltpu.roll(gate, rows - 1, 0)[halo:halo + tm], 0.0)
        g_mid = gate[halo:halo + tm]
        cw = cw_ref[:, c0:c0 + FFN_CHUNK]
        conv = g_prev * cw[0:1] + g_mid * cw[1:2] + g_next * cw[2:3] + cb_ref[:, c0:c0 + FFN_CHUNK]
        h_ref[:, c0:c0 + FFN_CHUNK] = (_gelu_tanh(conv) * val).astype(BF16)
    y = x + _dot(h_ref[...], wdn_ref[...])
    if final_norm:
        y = _rms(y, gf_ref[...], NORM_EPS)
    o_ref[...] = y


def _ffn(x, g, w_up, conv_w, conv_b, w_down, g_final, seq, tm, final_norm):
    tokens = x.shape[0]
    per_seq = seq // tm
    groups = tm // SUBLANES
    last_group = tokens // SUBLANES - 1
    return pl.pallas_call(
        functools.partial(_ffn_kernel, tm=tm, per_seq=per_seq, final_norm=final_norm),
        out_shape=jax.ShapeDtypeStruct((tokens, D_MODEL), F32),
        grid=(tokens // tm,),
        in_specs=[
            pl.BlockSpec((tm, D_MODEL), lambda i: (i, 0)),
            pl.BlockSpec((SUBLANES, D_MODEL), lambda i: (jnp.maximum(i * groups - 1, 0), 0)),
            pl.BlockSpec((SUBLANES, D_MODEL), lambda i: (jnp.minimum((i + 1) * groups, last_group), 0)),
            _resident((1, D_MODEL)),
            _resident((D_MODEL, 2 * D_FF)),
            _resident((3, D_FF)),
            _resident((1, D_FF)),
            _resident((D_FF, D_MODEL)),
            _resident((1, D_MODEL)),
        ],
        out_specs=pl.BlockSpec((tm, D_MODEL), lambda i: (i, 0)),
        scratch_shapes=[pltpu.VMEM((tm, D_FF), BF16)],
        compiler_params=_params("parallel"),
        name="ffn_convglu",
    )(x, x, x, g, w_up, conv_w, conv_b, w_down, g_final)


def _rope_tables(seq):
    half = DA_HEAD_DIM // 2
    inv_freq = ROPE_THETA ** (-jnp.arange(0, half, dtype=F32) * 2.0 / DA_HEAD_DIM)
    ang = jnp.arange(seq, dtype=F32)[:, None] * inv_freq[None, :]
    cos = jnp.tile(jnp.cos(ang), (1, 4))
    sin = jnp.sin(ang)
    sin = jnp.concatenate([-sin, -sin, sin, sin], axis=1)
    log2e = math.log2(math.e)
    return jnp.concatenate([cos * log2e, cos], axis=1), jnp.concatenate([sin * log2e, sin], axis=1)


def _dft_tables(n):
    idx = jnp.arange(n, dtype=jnp.int32)
    ang = ((idx[:, None] * idx[None, :]) % n).astype(F32) * (2.0 * math.pi / n)
    return jnp.cos(ang), jnp.sin(ang)


def _seq_dft_tables(seq):
    base = FNET_SEQ_BASE
    radix = seq // base
    assert radix * base == seq and base % radix == 0
    k2 = jnp.arange(base, dtype=jnp.int32)[None, :, None]
    n1 = jnp.arange(radix, dtype=jnp.int32)[:, None, None]
    n2 = jnp.arange(base, dtype=jnp.int32)[None, None, :]
    ang = ((k2 * (n1 + radix * n2)) % seq).astype(F32) * (2.0 * math.pi / seq)
    c, s = jnp.cos(ang), jnp.sin(ang)
    stage1 = jnp.concatenate([jnp.concatenate([c, -s], axis=2), jnp.concatenate([-s, -c], axis=2)], axis=1)
    gc, gs = _dft_tables(radix)
    eye = jnp.eye(base // radix, dtype=F32)
    stage2 = jnp.concatenate([jnp.kron(gc, eye), jnp.kron(gs, eye)], axis=1)
    return stage1.astype(BF16), stage2.astype(BF16)


def _rotary_head_layout(w):
    d_in = w.shape[0]
    w = w.reshape(d_in, DA_HEADS, 2, 2, DA_HEAD_DIM // 2)
    return w.transpose(0, 1, 3, 2, 4).reshape(d_in, D_MODEL)


def _qkv_weight(w_qkv):
    wq, wk, wv = jnp.split(w_qkv, 3, axis=1)
    wq = _rotary_head_layout(wq) * (DA_HEAD_DIM ** -0.5)
    return jnp.concatenate([wq, _rotary_head_layout(wk), wv], axis=1).astype(BF16)


def _trunk(x, mem, p):
    batch, seq, _ = x.shape
    mem_tokens = mem.shape[1]
    tokens = batch * seq
    tm = _token_tile(seq, 512)
    tm_wide = _token_tile(seq, 1024)
    x = x.reshape(tokens, D_MODEL)
    mem = mem.reshape(batch * mem_tokens, D_MODEL)

    for layer in range(DEPTH):
        j = layer // N_MIXERS
        g_mix = p["norm_mix_g"][layer][None]
        if layer % N_MIXERS == 0:
            lambda_init = 0.8 - 0.6 * math.exp(-0.3 * layer)
            cos, sin = _rope_tables(seq)
            tq = _token_tile(seq, 256)
            block = max(tq, min(seq, ATTN_SCORE_SCRATCH_BYTES // (2 * 4 * seq)) // tq * tq)
            assert seq % block == 0
            qkv = _qkv_rope(x, g_mix, p["w_qkv"][j], cos, sin, seq, tm_wide)
            a = _diff_attention(qkv, p["lq1"][j][None], p["lk1"][j][None], p["lq2"][j][None],
                                p["lk2"][j][None], p["subln_g"][j][None], batch, seq, lambda_init,
                                tq=tq, tk=_token_tile(seq, 256), parts=block // tq)
            w_mix_out = p["attn_w_o"][j]
        else:
            ccos, csin = _dft_tables(FNET_GROUP_DIM)
            ab = _fnet_in(x, g_mix, p["fnet_w_in"][j], jnp.concatenate([ccos, csin], axis=1).astype(BF16),
                          batch, seq, tm)
            a = _seq_dft(ab, *_seq_dft_tables(seq), batch, seq, cols=512)
            w_mix_out = p["fnet_w_out"][j]
        kv = _mem_kv(mem, p["norm_mem_g"][layer][None], p["xattn_w_kv"][layer], _token_tile(mem_tokens, 256))
        x = _post_mixer(x, a, w_mix_out, p["norm_xattn_g"][layer][None], p["xattn_w_q"][layer], kv,
                        p["xattn_w_o"][layer], seq, mem_tokens, tm_wide)
        x = _ffn(x, p["norm_ffn_g"][layer][None], p["ffn_w_up"][layer], p["ffn_conv_w"][layer],
                 p["ffn_conv_b"][layer][None], p["ffn_w_down"][layer], p["final_norm_g"][None],
                 seq, tm_wide, final_norm=(layer == DEPTH - 1))
    return x.reshape(batch, seq, D_MODEL)


def kernel(x_prompt, x_sample, mem_prompt, mem_sample, norm_mix_g, norm_xattn_g, norm_mem_g, norm_ffn_g, final_norm_g, attn_w_qkv, attn_lambda_q1, attn_lambda_k1, attn_lambda_q2, attn_lambda_k2, attn_subln_g, attn_w_o, fnet_w_in, fnet_w_out, xattn_w_q, xattn_w_kv, xattn_w_o, ffn_w_up, ffn_conv_w, ffn_conv_b, ffn_w_down):
    p = {
        "norm_mix_g": norm_mix_g, "norm_xattn_g": norm_xattn_g, "norm_mem_g": norm_mem_g,
        "norm_ffn_g": norm_ffn_g, "final_norm_g": final_norm_g,
        "w_qkv": jnp.stack([_qkv_weight(w) for w in attn_w_qkv]),
        "lq1": attn_lambda_q1, "lk1": attn_lambda_k1, "lq2": attn_lambda_q2, "lk2": attn_lambda_k2,
        "subln_g": attn_subln_g,
        "attn_w_o": attn_w_o.astype(BF16),
        "fnet_w_in": fnet_w_in.astype(BF16), "fnet_w_out": fnet_w_out.astype(BF16),
        "xattn_w_q": (xattn_w_q * (XA_HEAD_DIM ** -0.5)).astype(BF16),
        "xattn_w_kv": xattn_w_kv.astype(BF16), "xattn_w_o": xattn_w_o.astype(BF16),
        "ffn_w_up": ffn_w_up.astype(BF16), "ffn_conv_w": ffn_conv_w, "ffn_conv_b": ffn_conv_b,
        "ffn_w_down": ffn_w_down.astype(BF16),
    }
    return (_trunk(x_prompt, mem_prompt, p), _trunk(x_sample, mem_sample, p))
```

```python
import functools
import math

import jax
import jax.numpy as jnp
from jax import lax
from jax.experimental import pallas as pl
from jax.experimental.pallas import tpu as pltpu

D_MODEL = 1024
DEPTH = 2
N_MIXERS = 2
DA_HEADS = 8
DA_HEAD_DIM = 64
DA_V_DIM = 2 * DA_HEAD_DIM
ROPE_THETA = 10000.0
FNET_GROUPS = 4
FNET_GROUP_DIM = D_MODEL // FNET_GROUPS
FNET_SEQ_BASE = 256
XA_HEADS = 4
XA_HEAD_DIM = D_MODEL // XA_HEADS
D_FF = 2816
NORM_EPS = 1e-6
SUBLN_EPS = 1e-5

LANES = 128
SUBLANES = 8
VMEM_LIMIT_BYTES = 56 * 1024 * 1024
FFN_CHUNK = 256
ATTN_SCORE_SCRATCH_BYTES = 32 * 1024 * 1024

F32 = jnp.float32
BF16 = jnp.bfloat16


def _params(*semantics):
    return pltpu.CompilerParams(dimension_semantics=semantics, vmem_limit_bytes=VMEM_LIMIT_BYTES)


def _resident(shape):
    return pl.BlockSpec(shape, lambda *_: (0,) * len(shape), pipeline_mode=pl.Buffered(1))


def _rms(x, g, eps):
    return x * lax.rsqrt(jnp.mean(x * x, axis=-1, keepdims=True) + eps) * g


def _dot(a, b):
    return jnp.dot(a, b, preferred_element_type=F32)


def _token_tile(seq, want):
    tile = min(seq, want)
    assert seq % tile == 0 and tile % SUBLANES == 0
    return tile


def _qkv_kernel(x_ref, g_ref, w_ref, cos_ref, sin_ref, o_ref, *, chunk):
    xn = _rms(x_ref[...], g_ref[...], NORM_EPS).astype(BF16)
    for c0 in range(0, 3 * D_MODEL, chunk):
        y = _dot(xn, w_ref[:, c0:c0 + chunk])
        for j in range(0, chunk, LANES):
            blk = y[:, j:j + LANES]
            part = (c0 + j) // D_MODEL
            if part < 2:
                cos = cos_ref[:, part * LANES:(part + 1) * LANES]
                sin = sin_ref[:, part * LANES:(part + 1) * LANES]
                blk = blk * cos + pltpu.roll(blk, LANES // 2, 1) * sin
            o_ref[:, c0 + j:c0 + j + LANES] = blk.astype(o_ref.dtype)


def _qkv_rope(x, g, w, cos, sin, seq, tm):
    tokens = x.shape[0]
    n_pos = seq // tm
    return pl.pallas_call(
        functools.partial(_qkv_kernel, chunk=512),
        out_shape=jax.ShapeDtypeStruct((tokens, 3 * D_MODEL), BF16),
        grid=(tokens // tm,),
        in_specs=[
            pl.BlockSpec((tm, D_MODEL), lambda i: (i, 0)),
            _resident((1, D_MODEL)),
            _resident((D_MODEL, 3 * D_MODEL)),
            pl.BlockSpec((tm, 2 * LANES), lambda i: (i % n_pos, 0)),
            pl.BlockSpec((tm, 2 * LANES), lambda i: (i % n_pos, 0)),
        ],
        out_specs=pl.BlockSpec((tm, 3 * D_MODEL), lambda i: (i, 0)),
        compiler_params=_params("parallel"),
        name="qkv_rope",
    )(x, g, w, cos, sin)


def _diff_attn_kernel(q_ref, k_ref, v_ref, lq1_ref, lk1_ref, lq2_ref, lk2_ref, g_ref, o_ref,
                      vt_ref, s_ref, m_ref, acc_ref, l_ref, *, tq, tk, seq, n_q, parts, lambda_init):
    t = pl.program_id(0)
    cols = 2 * tq
    sub = SUBLANES

    @pl.when(t == 0)
    def _():
        s_ref[...] = jnp.zeros(s_ref.shape, F32)
        m_ref[...] = jnp.zeros(m_ref.shape, F32)
        acc_ref[...] = jnp.zeros(acc_ref.shape, F32)
        l_ref[...] = jnp.ones(l_ref.shape, F32)

    @pl.when(jnp.maximum(t - 1, 0) % n_q == 0)
    def _():
        vt_ref[...] = v_ref[...].astype(F32).T.astype(BF16)

    lam = (jnp.exp(jnp.sum(lq1_ref[...] * lk1_ref[...], axis=-1, keepdims=True))
           - jnp.exp(jnp.sum(lq2_ref[...] * lk2_ref[...], axis=-1, keepdims=True))
           + lambda_init)

    for part in range(parts):
        rows = slice(part * tq, (part + 1) * tq)
        span = slice(part * cols, (part + 1) * cols)

        q = q_ref[rows, :]
        lane = lax.broadcasted_iota(jnp.int32, q.shape, 1)
        first = (lane // (DA_HEAD_DIM // 2)) % 2 == 0
        zero = jnp.zeros_like(q)
        qs = jnp.concatenate([jnp.where(first, q, zero), jnp.where(first, zero, q)], axis=0)

        o = acc_ref[:, span] / jnp.sum(l_ref[:, span], axis=0, keepdims=True)
        o = o[:, :tq] - lam * o[:, tq:]
        o = o * lax.rsqrt(jnp.mean(o * o, axis=0, keepdims=True) + SUBLN_EPS) * g_ref[...] * (1.0 - lambda_init)
        o_ref[rows, :] = o.T.astype(o_ref.dtype)

        m_old = m_ref[:, span]
        mx = lsum = acc = None
        for off in range(0, seq, tk):
            s_new = lax.dot_general(k_ref[off:off + tk, :], qs, (((1,), (1,)), ((), ())),
                                    preferred_element_type=F32)
            s_old = s_ref[off:off + tk, span]
            chunk_max = jnp.max(s_new.reshape(tk // sub, sub, cols), axis=0)
            mx = chunk_max if mx is None else jnp.maximum(mx, chunk_max)
            p = jnp.exp2(s_old.reshape(tk // sub, sub, cols) - m_old)
            chunk_sum = jnp.sum(p, axis=0)
            lsum = chunk_sum if lsum is None else lsum + chunk_sum
            s_ref[off:off + tk, span] = s_new
            pv = _dot(vt_ref[:, off:off + tk], p.reshape(tk, cols).astype(BF16))
            acc = pv if acc is None else acc + pv

        acc_ref[:, span] = acc
        l_ref[:, span] = lsum
        m_ref[:, span] = jnp.broadcast_to(jnp.max(mx, axis=0, keepdims=True), (sub, cols))


def _diff_attention(qkv, lq1, lk1, lq2, lk2, subln_g, batch, seq, lambda_init, tq, tk, parts):
    block = parts * tq
    assert seq % block == 0
    n_q = seq // block
    n_tiles = batch * DA_HEADS * n_q

    def stage_tile(t, lag):
        return jnp.clip(t - lag, 0, n_tiles - 1)

    def scored(t):
        return stage_tile(t, 0)

    def applied(t):
        return stage_tile(t, 1)

    def written(t):
        return stage_tile(t, 2)

    def head_of(tile):
        return tile // (DA_HEADS * n_q), (tile // n_q) % DA_HEADS

    def q_block(tile):
        b, h = head_of(tile)
        return b * n_q + tile % n_q, h

    def k_block(t):
        b, h = head_of(scored(t))
        return b, DA_HEADS + h

    def v_block(t):
        b, h = head_of(applied(t))
        return b, 2 * DA_HEADS + h

    vec = _resident((1, DA_HEAD_DIM))
    cols = 2 * block
    return pl.pallas_call(
        functools.partial(_diff_attn_kernel, tq=tq, tk=tk, seq=seq, n_q=n_q, parts=parts,
                          lambda_init=lambda_init),
        out_shape=jax.ShapeDtypeStruct((batch * seq, D_MODEL), BF16),
        grid=(n_tiles + 2,),
        in_specs=[
            pl.BlockSpec((block, DA_V_DIM), lambda t: q_block(scored(t))),
            pl.BlockSpec((seq, DA_V_DIM), k_block),
            pl.BlockSpec((seq, DA_V_DIM), v_block),
            vec, vec, vec, vec,
            _resident((DA_V_DIM, 1)),
        ],
        out_specs=pl.BlockSpec((block, DA_V_DIM), lambda t: q_block(written(t))),
        scratch_shapes=[
            pltpu.VMEM((DA_V_DIM, seq), BF16),
            pltpu.VMEM((seq, cols), F32),
            pltpu.VMEM((SUBLANES, cols), F32),
            pltpu.VMEM((DA_V_DIM, cols), F32),
            pltpu.VMEM((SUBLANES, cols), F32),
        ],
        compiler_params=_params("arbitrary"),
        name="diff_attention",
    )(qkv, qkv, qkv, lq1, lk1, lq2, lk2, subln_g.reshape(DA_V_DIM, 1))


def _fnet_in_kernel(x_ref, g_ref, w_ref, dft_ref, o_ref, *, radix, per_residue):
    x = jnp.concatenate([x_ref[:, n1, :] for n1 in range(radix)], axis=0)
    xn = _rms(x, g_ref[...], NORM_EPS).astype(BF16)
    u = _dot(xn, w_ref[...]).astype(BF16)
    dft = dft_ref[...]
    for grp in range(FNET_GROUPS):
        lo = grp * FNET_GROUP_DIM
        ab = _dot(u[:, lo:lo + FNET_GROUP_DIM], dft).astype(o_ref.dtype)
        for n1 in range(radix):
            rows = slice(n1 * per_residue, (n1 + 1) * per_residue)
            o_ref[n1, :, lo:lo + FNET_GROUP_DIM] = ab[rows, :FNET_GROUP_DIM]
            o_ref[n1, :, D_MODEL + lo:D_MODEL + lo + FNET_GROUP_DIM] = ab[rows, FNET_GROUP_DIM:]


def _fnet_in(x, g, w, dft, batch, seq, tm):
    radix = seq // FNET_SEQ_BASE
    per_residue = tm // radix
    tiles_per_seq = seq // tm
    return pl.pallas_call(
        functools.partial(_fnet_in_kernel, radix=radix, per_residue=per_residue),
        out_shape=jax.ShapeDtypeStruct((batch * radix, FNET_SEQ_BASE, 2 * D_MODEL), BF16),
        grid=(batch * tiles_per_seq,),
        in_specs=[
            pl.BlockSpec((per_residue, radix, D_MODEL), lambda i: (i, 0, 0)),
            _resident((1, D_MODEL)),
            _resident((D_MODEL, D_MODEL)),
            _resident((FNET_GROUP_DIM, 2 * FNET_GROUP_DIM)),
        ],
        out_specs=pl.BlockSpec((radix, per_residue, 2 * D_MODEL),
                               lambda i: (i // tiles_per_seq, i % tiles_per_seq, 0)),
        compiler_params=_params("parallel"),
        name="fnet_in",
    )(x.reshape(x.shape[0] // radix, radix, D_MODEL), g, w, dft)


def _seq_dft_kernel(a_ref, b_ref, stage1_ref, stage2_ref, o_ref, y_ref, *, radix, scale):
    base = FNET_SEQ_BASE
    for n1 in range(radix):
        m = stage1_ref[n1]
        y = (_dot(m[:, :base], a_ref[n1]) + _dot(m[:, base:], b_ref[n1])).astype(BF16)
        y_ref[0, n1] = y[:base]
        y_ref[1, n1] = y[base:]
    kb = base // radix
    g = stage2_ref[...]
    cols = y_ref.shape[-1]
    for k2 in range(0, base, kb):
        yr = y_ref[0, :, k2:k2 + kb, :].reshape(base, cols)
        yi = y_ref[1, :, k2:k2 + kb, :].reshape(base, cols)
        out = (_dot(g[:, :base], yr) + _dot(g[:, base:], yi)) * scale
        o_ref[:, k2:k2 + kb, :] = out.reshape(radix, kb, cols).astype(o_ref.dtype)


def _seq_dft(ab, stage1, stage2, batch, seq, cols):
    radix = seq // FNET_SEQ_BASE
    col_blocks = D_MODEL // cols
    scale = 1.0 / math.sqrt(seq * FNET_GROUP_DIM)
    out = pl.pallas_call(
        functools.partial(_seq_dft_kernel, radix=radix, scale=scale),
        out_shape=jax.ShapeDtypeStruct((batch * radix, FNET_SEQ_BASE, D_MODEL), BF16),
        grid=(batch, col_blocks),
        in_specs=[
            pl.BlockSpec((radix, FNET_SEQ_BASE, cols), lambda b, c: (b, 0, c)),
            pl.BlockSpec((radix, FNET_SEQ_BASE, cols), lambda b, c: (b, 0, col_blocks + c)),
            _resident((radix, 2 * FNET_SEQ_BASE, 2 * FNET_SEQ_BASE)),
            _resident((FNET_SEQ_BASE, 2 * FNET_SEQ_BASE)),
        ],
        out_specs=pl.BlockSpec((radix, FNET_SEQ_BASE, cols), lambda b, c: (b, 0, c)),
        scratch_shapes=[pltpu.VMEM((2, radix, FNET_SEQ_BASE, cols), BF16)],
        compiler_params=_params("parallel", "parallel"),
        name="seq_dft",
    )(ab, ab, stage1, stage2)
    return out.reshape(batch * seq, D_MODEL)


def _mem_kv_kernel(m_ref, g_ref, w_ref, o_ref):
    mn = _rms(m_ref[...], g_ref[...], NORM_EPS).astype(BF16)
    o_ref[...] = _dot(mn, w_ref[...]).astype(o_ref.dtype)


def _mem_kv(mem, g, w, tm):
    rows = mem.shape[0]
    return pl.pallas_call(
        _mem_kv_kernel,
        out_shape=jax.ShapeDtypeStruct((rows, 2 * D_MODEL), BF16),
        grid=(rows // tm,),
        in_specs=[
            pl.BlockSpec((tm, D_MODEL), lambda i: (i, 0)),
            _resident((1, D_MODEL)),
            _resident((D_MODEL, 2 * D_MODEL)),
        ],
        out_specs=pl.BlockSpec((tm, 2 * D_MODEL), lambda i: (i, 0)),
        compiler_params=_params("parallel"),
        name="mem_kv",
    )(mem, g, w)


def _post_mixer_kernel(x_ref, a_ref, wm_ref, g_ref, wq_ref, kv_ref, wo_ref, o_ref, heads_ref):
    x1 = x_ref[...] + _dot(a_ref[...], wm_ref[...])
    xn = _rms(x1, g_ref[...], NORM_EPS).astype(BF16)
    q = _dot(xn, wq_ref[...]).astype(BF16)
    for h in range(XA_HEADS):
        lo = h * XA_HEAD_DIM
        kh = kv_ref[:, lo:lo + XA_HEAD_DIM]
        vh = kv_ref[:, D_MODEL + lo:D_MODEL + lo + XA_HEAD_DIM]
        s = lax.dot_general(q[:, lo:lo + XA_HEAD_DIM], kh, (((1,), (1,)), ((), ())),
                            preferred_element_type=F32)
        p = jnp.exp(s - jnp.max(s, axis=-1, keepdims=True))
        denom = jnp.sum(p, axis=-1, keepdims=True)
        heads_ref[:, lo:lo + XA_HEAD_DIM] = (_dot(p.astype(BF16), vh) / denom).astype(BF16)
    o_ref[...] = x1 + _dot(heads_ref[...], wo_ref[...])


def _post_mixer(x, a, w_mix_out, g, w_q, kv, w_o, seq, mem_tokens, tm):
    tokens = x.shape[0]
    per_seq = seq // tm
    square = _resident((D_MODEL, D_MODEL))
    return pl.pallas_call(
        _post_mixer_kernel,
        out_shape=jax.ShapeDtypeStruct((tokens, D_MODEL), F32),
        grid=(tokens // tm,),
        in_specs=[
            pl.BlockSpec((tm, D_MODEL), lambda i: (i, 0)),
            pl.BlockSpec((tm, D_MODEL), lambda i: (i, 0)),
            square,
            _resident((1, D_MODEL)),
            square,
            pl.BlockSpec((mem_tokens, 2 * D_MODEL), lambda i: (i // per_seq, 0)),
            square,
        ],
        out_specs=pl.BlockSpec((tm, D_MODEL), lambda i: (i, 0)),
        scratch_shapes=[pltpu.VMEM((tm, D_MODEL), BF16)],
        compiler_params=_params("parallel"),
        name="post_mixer_xattn",
    )(x, a, w_mix_out, g, w_q, kv, w_o)


def _gelu_tanh(x):
    return x * (0.5 * (1.0 + jnp.tanh(math.sqrt(2.0 / math.pi) * (x + 0.044715 * (x * x * x)))))


def _ffn_kernel(x_ref, prev_ref, next_ref, g_ref, wup_ref, cw_ref, cb_ref, wdn_ref, gf_ref, o_ref,
                h_ref, *, tm, per_seq, final_norm):
    i = pl.program_id(0)
    x = x_ref[...]
    halo = SUBLANES
    xcat = jnp.concatenate([prev_ref[...], x, next_ref[...]], axis=0)
    xn = _rms(xcat, g_ref[...], NORM_EPS).astype(BF16)
    xn_mid = xn[halo:halo + tm]
    row = lax.broadcasted_iota(jnp.int32, (tm, 1), 0)
    keep_prev = jnp.logical_or(row > 0, i % per_seq != 0)
    keep_next = jnp.logical_or(row < tm - 1, i % per_seq != per_seq - 1)
    rows = tm + 2 * halo
    for c0 in range(0, D_FF, FFN_CHUNK):
        gate = _dot(xn, wup_ref[:, c0:c0 + FFN_CHUNK])
        val = _dot(xn_mid, wup_ref[:, D_FF + c0:D_FF + c0 + FFN_CHUNK])
        g_prev = jnp.where(keep_prev, pltpu.roll(gate, 1, 0)[halo:halo + tm], 0.0)
        g_next = jnp.where(keep_next, pltpu.roll(gate, rows - 1, 0)[halo:halo + tm], 0.0)
        g_mid = gate[halo:halo + tm]
        cw = cw_ref[:, c0:c0 + FFN_CHUNK]
        conv = g_prev * cw[0:1] + g_mid * cw[1:2] + g_next * cw[2:3] + cb_ref[:, c0:c0 + FFN_CHUNK]
        h_ref[:, c0:c0 + FFN_CHUNK] = (_gelu_tanh(conv) * val).astype(BF16)
    y = x + _dot(h_ref[...], wdn_ref[...])
    if final_norm:
        y = _rms(y, gf_ref[...], NORM_EPS)
    o_ref[...] = y


def _ffn(x, g, w_up, conv_w, conv_b, w_down, g_final, seq, tm, final_norm):
    tokens = x.shape[0]
    per_seq = seq // tm
    groups = tm // SUBLANES
    last_group = tokens // SUBLANES - 1
    return pl.pallas_call(
        functools.partial(_ffn_kernel, tm=tm, per_seq=per_seq, final_norm=final_norm),
        out_shape=jax.ShapeDtypeStruct((tokens, D_MODEL), F32),
        grid=(tokens // tm,),
        in_specs=[
            pl.BlockSpec((tm, D_MODEL), lambda i: (i, 0)),
            pl.BlockSpec((SUBLANES, D_MODEL), lambda i: (jnp.maximum(i * groups - 1, 0), 0)),
            pl.BlockSpec((SUBLANES, D_MODEL), lambda i: (jnp.minimum((i + 1) * groups, last_group), 0)),
            _resident((1, D_MODEL)),
            _resident((D_MODEL, 2 * D_FF)),
            _resident((3, D_FF)),
            _resident((1, D_FF)),
            _resident((D_FF, D_MODEL)),
            _resident((1, D_MODEL)),
        ],
        out_specs=pl.BlockSpec((tm, D_MODEL), lambda i: (i, 0)),
        scratch_shapes=[pltpu.VMEM((tm, D_FF), BF16)],
        compiler_params=_params("parallel"),
        name="ffn_convglu",
    )(x, x, x, g, w_up, conv_w, conv_b, w_down, g_final)


def _rope_tables(seq):
    half = DA_HEAD_DIM // 2
    inv_freq = ROPE_THETA ** (-jnp.arange(0, half, dtype=F32) * 2.0 / DA_HEAD_DIM)
    ang = jnp.arange(seq, dtype=F32)[:, None] * inv_freq[None, :]
    cos = jnp.tile(jnp.cos(ang), (1, 4))
    sin = jnp.sin(ang)
    sin = jnp.concatenate([-sin, -sin, sin, sin], axis=1)
    log2e = math.log2(math.e)
    return jnp.concatenate([cos * log2e, cos], axis=1), jnp.concatenate([sin * log2e, sin], axis=1)


def _dft_tables(n):
    idx = jnp.arange(n, dtype=jnp.int32)
    ang = ((idx[:, None] * idx[None, :]) % n).astype(F32) * (2.0 * math.pi / n)
    return jnp.cos(ang), jnp.sin(ang)


def _seq_dft_tables(seq):
    base = FNET_SEQ_BASE
    radix = seq // base
    assert radix * base == seq and base % radix == 0
    k2 = jnp.arange(base, dtype=jnp.int32)[None, :, None]
    n1 = jnp.arange(radix, dtype=jnp.int32)[:, None, None]
    n2 = jnp.arange(base, dtype=jnp.int32)[None, None, :]
    ang = ((k2 * (n1 + radix * n2)) % seq).astype(F32) * (2.0 * math.pi / seq)
    c, s = jnp.cos(ang), jnp.sin(ang)
    stage1 = jnp.concatenate([jnp.concatenate([c, -s], axis=2), jnp.concatenate([-s, -c], axis=2)], axis=1)
    gc, gs = _dft_tables(radix)
    eye = jnp.eye(base // radix, dtype=F32)
    stage2 = jnp.concatenate([jnp.kron(gc, eye), jnp.kron(gs, eye)], axis=1)
    return stage1.astype(BF16), stage2.astype(BF16)


def _rotary_head_layout(w):
    d_in = w.shape[0]
    w = w.reshape(d_in, DA_HEADS, 2, 2, DA_HEAD_DIM // 2)
    return w.transpose(0, 1, 3, 2, 4).reshape(d_in, D_MODEL)


def _qkv_weight(w_qkv):
    wq, wk, wv = jnp.split(w_qkv, 3, axis=1)
    wq = _rotary_head_layout(wq) * (DA_HEAD_DIM ** -0.5)
    return jnp.concatenate([wq, _rotary_head_layout(wk), wv], axis=1).astype(BF16)


def _trunk(x, mem, p):
    batch, seq, _ = x.shape
    mem_tokens = mem.shape[1]
    tokens = batch * seq
    tm = _token_tile(seq, 1024)
    x = x.reshape(tokens, D_MODEL)
    mem = mem.reshape(batch * mem_tokens, D_MODEL)

    for layer in range(DEPTH):
        j = layer // N_MIXERS
        g_mix = p["norm_mix_g"][layer][None]
        if layer % N_MIXERS == 0:
            lambda_init = 0.8 - 0.6 * math.exp(-0.3 * layer)
            cos, sin = _rope_tables(seq)
            tq = _token_tile(seq, 256)
            block = max(tq, min(seq, ATTN_SCORE_SCRATCH_BYTES // (2 * 4 * seq)) // tq * tq)
            assert seq % block == 0
            qkv = _qkv_rope(x, g_mix, p["w_qkv"][j], cos, sin, seq, tm)
            a = _diff_attention(qkv, p["lq1"][j][None], p["lk1"][j][None], p["lq2"][j][None],
                                p["lk2"][j][None], p["subln_g"][j][None], batch, seq, lambda_init,
                                tq=tq, tk=_token_tile(seq, 256), parts=block // tq)
            w_mix_out = p["attn_w_o"][j]
        else:
            ccos, csin = _dft_tables(FNET_GROUP_DIM)
            ab = _fnet_in(x, g_mix, p["fnet_w_in"][j], jnp.concatenate([ccos, csin], axis=1).astype(BF16),
                          batch, seq, tm)
            a = _seq_dft(ab, *_seq_dft_tables(seq), batch, seq, cols=512)
            w_mix_out = p["fnet_w_out"][j]
        kv = _mem_kv(mem, p["norm_mem_g"][layer][None], p["xattn_w_kv"][layer],
                     _token_tile(batch * mem_tokens, 1024))
        x = _post_mixer(x, a, w_mix_out, p["norm_xattn_g"][layer][None], p["xattn_w_q"][layer], kv,
                        p["xattn_w_o"][layer], seq, mem_tokens, tm)
        x = _ffn(x, p["norm_ffn_g"][layer][None], p["ffn_w_up"][layer], p["ffn_conv_w"][layer],
                 p["ffn_conv_b"][layer][None], p["ffn_w_down"][layer], p["final_norm_g"][None],
                 seq, tm, final_norm=(layer == DEPTH - 1))
    return x.reshape(batch, seq, D_MODEL)


def kernel(x_prompt, x_sample, mem_prompt, mem_sample, norm_mix_g, norm_xattn_g, norm_mem_g, norm_ffn_g, final_norm_g, attn_w_qkv, attn_lambda_q1, attn_lambda_k1, attn_lambda_q2, attn_lambda_k2, attn_subln_g, attn_w_o, fnet_w_in, fnet_w_out, xattn_w_q, xattn_w_kv, xattn_w_o, ffn_w_up, ffn_conv_w, ffn_conv_b, ffn_w_down):
    p = {
        "norm_mix_g": norm_mix_g, "norm_xattn_g": norm_xattn_g, "norm_mem_g": norm_mem_g,
        "norm_ffn_g": norm_ffn_g, "final_norm_g": final_norm_g,
        "w_qkv": jnp.stack([_qkv_weight(w) for w in attn_w_qkv]),
        "lq1": attn_lambda_q1, "lk1": attn_lambda_k1, "lq2": attn_lambda_q2, "lk2": attn_lambda_k2,
        "subln_g": attn_subln_g,
        "attn_w_o": attn_w_o.astype(BF16),
        "fnet_w_in": fnet_w_in.astype(BF16), "fnet_w_out": fnet_w_out.astype(BF16),
        "xattn_w_q": (xattn_w_q * (XA_HEAD_DIM ** -0.5)).astype(BF16),
        "xattn_w_kv": xattn_w_kv.astype(BF16), "xattn_w_o": xattn_w_o.astype(BF16),
        "ffn_w_up": ffn_w_up.astype(BF16), "ffn_conv_w": ffn_conv_w, "ffn_conv_b": ffn_conv_b,
        "ffn_w_down": ffn_w_down.astype(BF16),
    }
    return (_trunk(x_prompt, mem_prompt, p), _trunk(x_sample, mem_sample, p))
```

```python
import functools
import math

import jax
import jax.numpy as jnp
from jax import lax
from jax.experimental import pallas as pl
from jax.experimental.pallas import tpu as pltpu

D_MODEL = 1024
DEPTH = 2
N_MIXERS = 2
DA_HEADS = 8
DA_HEAD_DIM = 64
DA_V_DIM = 2 * DA_HEAD_DIM
ROPE_THETA = 10000.0
FNET_GROUPS = 4
FNET_GROUP_DIM = D_MODEL // FNET_GROUPS
FNET_SEQ_BASE = 256
XA_HEADS = 4
XA_HEAD_DIM = D_MODEL // XA_HEADS
D_FF = 2816
NORM_EPS = 1e-6
SUBLN_EPS = 1e-5

LANES = 128
SUBLANES = 8
VMEM_LIMIT_BYTES = 56 * 1024 * 1024
FFN_CHUNK = 256
ATTN_SCORE_SCRATCH_BYTES = 32 * 1024 * 1024

F32 = jnp.float32
BF16 = jnp.bfloat16


def _params(*semantics):
    return pltpu.CompilerParams(dimension_semantics=semantics, vmem_limit_bytes=VMEM_LIMIT_BYTES)


def _resident(shape):
    return pl.BlockSpec(shape, lambda *_: (0,) * len(shape), pipeline_mode=pl.Buffered(1))


def _rms(x, g, eps):
    return x * lax.rsqrt(jnp.mean(x * x, axis=-1, keepdims=True) + eps) * g


def _dot(a, b):
    return jnp.dot(a, b, preferred_element_type=F32)


def _token_tile(seq, want):
    tile = min(seq, want)
    assert seq % tile == 0 and tile % SUBLANES == 0
    return tile


def _qkv_kernel(x_ref, g_ref, w_ref, cos_ref, sin_ref, o_ref, *, chunk):
    xn = _rms(x_ref[...], g_ref[...], NORM_EPS).astype(BF16)
    for c0 in range(0, 3 * D_MODEL, chunk):
        y = _dot(xn, w_ref[:, c0:c0 + chunk])
        for j in range(0, chunk, LANES):
            blk = y[:, j:j + LANES]
            part = (c0 + j) // D_MODEL
            if part < 2:
                cos = cos_ref[:, part * LANES:(part + 1) * LANES]
                sin = sin_ref[:, part * LANES:(part + 1) * LANES]
                blk = blk * cos + pltpu.roll(blk, LANES // 2, 1) * sin
            o_ref[:, c0 + j:c0 + j + LANES] = blk.astype(o_ref.dtype)


def _qkv_rope(x, g, w, cos, sin, seq, tm):
    tokens = x.shape[0]
    n_pos = seq // tm
    return pl.pallas_call(
        functools.partial(_qkv_kernel, chunk=512),
        out_shape=jax.ShapeDtypeStruct((tokens, 3 * D_MODEL), BF16),
        grid=(tokens // tm,),
        in_specs=[
            pl.BlockSpec((tm, D_MODEL), lambda i: (i, 0)),
            _resident((1, D_MODEL)),
            _resident((D_MODEL, 3 * D_MODEL)),
            pl.BlockSpec((tm, 2 * LANES), lambda i: (i % n_pos, 0)),
            pl.BlockSpec((tm, 2 * LANES), lambda i: (i % n_pos, 0)),
        ],
        out_specs=pl.BlockSpec((tm, 3 * D_MODEL), lambda i: (i, 0)),
        compiler_params=_params("parallel"),
        name="qkv_rope",
    )(x, g, w, cos, sin)


def _diff_attn_kernel(q_ref, k_ref, v_ref, lq1_ref, lk1_ref, lq2_ref, lk2_ref, g_ref, o_ref,
                      vt_ref, s_ref, m_ref, acc_ref, l_ref, *, tq, tk, seq, n_q, parts, lambda_init):
    t = pl.program_id(0)
    cols = 2 * tq
    sub = SUBLANES

    @pl.when(t == 0)
    def _():
        s_ref[...] = jnp.zeros(s_ref.shape, F32)
        m_ref[...] = jnp.zeros(m_ref.shape, F32)
        acc_ref[...] = jnp.zeros(acc_ref.shape, F32)
        l_ref[...] = jnp.ones(l_ref.shape, F32)

    @pl.when(jnp.maximum(t - 1, 0) % n_q == 0)
    def _():
        vt_ref[...] = v_ref[...].astype(F32).T.astype(BF16)

    lam = (jnp.exp(jnp.sum(lq1_ref[...] * lk1_ref[...], axis=-1, keepdims=True))
           - jnp.exp(jnp.sum(lq2_ref[...] * lk2_ref[...], axis=-1, keepdims=True))
           + lambda_init)

    for part in range(parts):
        rows = slice(part * tq, (part + 1) * tq)
        span = slice(part * cols, (part + 1) * cols)

        q = q_ref[rows, :]
        lane = lax.broadcasted_iota(jnp.int32, q.shape, 1)
        first = (lane // (DA_HEAD_DIM // 2)) % 2 == 0
        zero = jnp.zeros_like(q)
        qs = jnp.concatenate([jnp.where(first, q, zero), jnp.where(first, zero, q)], axis=0)

        o = acc_ref[:, span] / jnp.sum(l_ref[:, span], axis=0, keepdims=True)
        o = o[:, :tq] - lam * o[:, tq:]
        o = o * lax.rsqrt(jnp.mean(o * o, axis=0, keepdims=True) + SUBLN_EPS) * g_ref[...] * (1.0 - lambda_init)
        o_ref[rows, :] = o.T.astype(o_ref.dtype)

        m_old = m_ref[:, span]
        mx = lsum = acc = None
        for off in range(0, seq, tk):
            s_new = lax.dot_general(k_ref[off:off + tk, :], qs, (((1,), (1,)), ((), ())),
                                    preferred_element_type=F32)
            s_old = s_ref[part, off:off + tk, :]
            chunk_max = jnp.max(s_new.reshape(tk // sub, sub, cols), axis=0)
            mx = chunk_max if mx is None else jnp.maximum(mx, chunk_max)
            p = jnp.exp2(s_old.reshape(tk // sub, sub, cols) - m_old)
            chunk_sum = jnp.sum(p, axis=0)
            lsum = chunk_sum if lsum is None else lsum + chunk_sum
            s_ref[part, off:off + tk, :] = s_new
            pv = _dot(vt_ref[:, off:off + tk], p.reshape(tk, cols).astype(BF16))
            acc = pv if acc is None else acc + pv

        acc_ref[:, span] = acc
        l_ref[:, span] = lsum
        m_ref[:, span] = jnp.broadcast_to(jnp.max(mx, axis=0, keepdims=True), (sub, cols))


def _diff_attention(qkv, lq1, lk1, lq2, lk2, subln_g, batch, seq, lambda_init, tq, tk, parts):
    block = parts * tq
    assert seq % block == 0
    n_q = seq // block
    n_tiles = batch * DA_HEADS * n_q

    def stage_tile(t, lag):
        return jnp.clip(t - lag, 0, n_tiles - 1)

    def scored(t):
        return stage_tile(t, 0)

    def applied(t):
        return stage_tile(t, 1)

    def written(t):
        return stage_tile(t, 2)

    def head_of(tile):
        return tile // (DA_HEADS * n_q), (tile // n_q) % DA_HEADS

    def q_block(tile):
        b, h = head_of(tile)
        return b * n_q + tile % n_q, h

    def k_block(t):
        b, h = head_of(scored(t))
        return b, DA_HEADS + h

    def v_block(t):
        b, h = head_of(applied(t))
        return b, 2 * DA_HEADS + h

    vec = _resident((1, DA_HEAD_DIM))
    cols = 2 * block
    return pl.pallas_call(
        functools.partial(_diff_attn_kernel, tq=tq, tk=tk, seq=seq, n_q=n_q, parts=parts,
                          lambda_init=lambda_init),
        out_shape=jax.ShapeDtypeStruct((batch * seq, D_MODEL), BF16),
        grid=(n_tiles + 2,),
        in_specs=[
            pl.BlockSpec((block, DA_V_DIM), lambda t: q_block(scored(t))),
            pl.BlockSpec((seq, DA_V_DIM), k_block),
            pl.BlockSpec((seq, DA_V_DIM), v_block),
            vec, vec, vec, vec,
            _resident((DA_V_DIM, 1)),
        ],
        out_specs=pl.BlockSpec((block, DA_V_DIM), lambda t: q_block(written(t))),
        scratch_shapes=[
            pltpu.VMEM((DA_V_DIM, seq), BF16),
            pltpu.VMEM((parts, seq, 2 * tq), F32),
            pltpu.VMEM((SUBLANES, cols), F32),
            pltpu.VMEM((DA_V_DIM, cols), F32),
            pltpu.VMEM((SUBLANES, cols), F32),
        ],
        compiler_params=_params("arbitrary"),
        name="diff_attention",
    )(qkv, qkv, qkv, lq1, lk1, lq2, lk2, subln_g.reshape(DA_V_DIM, 1))


def _fnet_in_kernel(x_ref, g_ref, w_ref, dft_ref, o_ref, *, radix, per_residue):
    x = jnp.concatenate([x_ref[:, n1, :] for n1 in range(radix)], axis=0)
    xn = _rms(x, g_ref[...], NORM_EPS).astype(BF16)
    u = _dot(xn, w_ref[...]).astype(BF16)
    dft = dft_ref[...]
    for grp in range(FNET_GROUPS):
        lo = grp * FNET_GROUP_DIM
        ab = _dot(u[:, lo:lo + FNET_GROUP_DIM], dft).astype(o_ref.dtype)
        for n1 in range(radix):
            rows = slice(n1 * per_residue, (n1 + 1) * per_residue)
            o_ref[n1, :, lo:lo + FNET_GROUP_DIM] = ab[rows, :FNET_GROUP_DIM]
            o_ref[n1, :, D_MODEL + lo:D_MODEL + lo + FNET_GROUP_DIM] = ab[rows, FNET_GROUP_DIM:]


def _fnet_in(x, g, w, dft, batch, seq, tm):
    radix = seq // FNET_SEQ_BASE
    per_residue = tm // radix
    tiles_per_seq = seq // tm
    return pl.pallas_call(
        functools.partial(_fnet_in_kernel, radix=radix, per_residue=per_residue),
        out_shape=jax.ShapeDtypeStruct((batch * radix, FNET_SEQ_BASE, 2 * D_MODEL), BF16),
        grid=(batch * tiles_per_seq,),
        in_specs=[
            pl.BlockSpec((per_residue, radix, D_MODEL), lambda i: (i, 0, 0)),
            _resident((1, D_MODEL)),
            _resident((D_MODEL, D_MODEL)),
            _resident((FNET_GROUP_DIM, 2 * FNET_GROUP_DIM)),
        ],
        out_specs=pl.BlockSpec((radix, per_residue, 2 * D_MODEL),
                               lambda i: (i // tiles_per_seq, i % tiles_per_seq, 0)),
        compiler_params=_params("parallel"),
        name="fnet_in",
    )(x.reshape(x.shape[0] // radix, radix, D_MODEL), g, w, dft)


def _seq_dft_kernel(a_ref, b_ref, stage1_ref, stage2_ref, o_ref, y_ref, *, radix, scale):
    base = FNET_SEQ_BASE
    for n1 in range(radix):
        m = stage1_ref[n1]
        y = (_dot(m[:, :base], a_ref[n1]) + _dot(m[:, base:], b_ref[n1])).astype(BF16)
        y_ref[0, n1] = y[:base]
        y_ref[1, n1] = y[base:]
    kb = base // radix
    g = stage2_ref[...]
    cols = y_ref.shape[-1]
    for k2 in range(0, base, kb):
        yr = y_ref[0, :, k2:k2 + kb, :].reshape(base, cols)
        yi = y_ref[1, :, k2:k2 + kb, :].reshape(base, cols)
        out = (_dot(g[:, :base], yr) + _dot(g[:, base:], yi)) * scale
        o_ref[:, k2:k2 + kb, :] = out.reshape(radix, kb, cols).astype(o_ref.dtype)


def _seq_dft(ab, stage1, stage2, batch, seq, cols):
    radix = seq // FNET_SEQ_BASE
    col_blocks = D_MODEL // cols
    scale = 1.0 / math.sqrt(seq * FNET_GROUP_DIM)
    out = pl.pallas_call(
        functools.partial(_seq_dft_kernel, radix=radix, scale=scale),
        out_shape=jax.ShapeDtypeStruct((batch * radix, FNET_SEQ_BASE, D_MODEL), BF16),
        grid=(batch, col_blocks),
        in_specs=[
            pl.BlockSpec((radix, FNET_SEQ_BASE, cols), lambda b, c: (b, 0, c)),
            pl.BlockSpec((radix, FNET_SEQ_BASE, cols), lambda b, c: (b, 0, col_blocks + c)),
            _resident((radix, 2 * FNET_SEQ_BASE, 2 * FNET_SEQ_BASE)),
            _resident((FNET_SEQ_BASE, 2 * FNET_SEQ_BASE)),
        ],
        out_specs=pl.BlockSpec((radix, FNET_SEQ_BASE, cols), lambda b, c: (b, 0, c)),
        scratch_shapes=[pltpu.VMEM((2, radix, FNET_SEQ_BASE, cols), BF16)],
        compiler_params=_params("parallel", "parallel"),
        name="seq_dft",
    )(ab, ab, stage1, stage2)
    return out.reshape(batch * seq, D_MODEL)


def _mem_kv_kernel(m_ref, g_ref, w_ref, o_ref):
    mn = _rms(m_ref[...], g_ref[...], NORM_EPS).astype(BF16)
    o_ref[...] = _dot(mn, w_ref[...]).astype(o_ref.dtype)


def _mem_kv(mem, g, w, tm):
    rows = mem.shape[0]
    return pl.pallas_call(
        _mem_kv_kernel,
        out_shape=jax.ShapeDtypeStruct((rows, 2 * D_MODEL), BF16),
        grid=(rows // tm,),
        in_specs=[
            pl.BlockSpec((tm, D_MODEL), lambda i: (i, 0)),
            _resident((1, D_MODEL)),
            _resident((D_MODEL, 2 * D_MODEL)),
        ],
        out_specs=pl.BlockSpec((tm, 2 * D_MODEL), lambda i: (i, 0)),
        compiler_params=_params("parallel"),
        name="mem_kv",
    )(mem, g, w)


def _post_mixer_kernel(x_ref, a_ref, wm_ref, g_ref, wq_ref, kv_ref, wo_ref, o_ref, heads_ref):
    x1 = x_ref[...] + _dot(a_ref[...], wm_ref[...])
    xn = _rms(x1, g_ref[...], NORM_EPS).astype(BF16)
    q = _dot(xn, wq_ref[...]).astype(BF16)
    for h in range(XA_HEADS):
        lo = h * XA_HEAD_DIM
        kh = kv_ref[:, lo:lo + XA_HEAD_DIM]
        vh = kv_ref[:, D_MODEL + lo:D_MODEL + lo + XA_HEAD_DIM]
        s = lax.dot_general(q[:, lo:lo + XA_HEAD_DIM], kh, (((1,), (1,)), ((), ())),
                            preferred_element_type=F32)
        p = jnp.exp(s - jnp.max(s, axis=-1, keepdims=True))
        denom = jnp.sum(p, axis=-1, keepdims=True)
        heads_ref[:, lo:lo + XA_HEAD_DIM] = (_dot(p.astype(BF16), vh) / denom).astype(BF16)
    o_ref[...] = x1 + _dot(heads_ref[...], wo_ref[...])


def _post_mixer(x, a, w_mix_out, g, w_q, kv, w_o, seq, mem_tokens, tm):
    tokens = x.shape[0]
    per_seq = seq // tm
    square = _resident((D_MODEL, D_MODEL))
    return pl.pallas_call(
        _post_mixer_kernel,
        out_shape=jax.ShapeDtypeStruct((tokens, D_MODEL), F32),
        grid=(tokens // tm,),
        in_specs=[
            pl.BlockSpec((tm, D_MODEL), lambda i: (i, 0)),
            pl.BlockSpec((tm, D_MODEL), lambda i: (i, 0)),
            square,
            _resident((1, D_MODEL)),
            square,
            pl.BlockSpec((mem_tokens, 2 * D_MODEL), lambda i: (i // per_seq, 0)),
            square,
        ],
        out_specs=pl.BlockSpec((tm, D_MODEL), lambda i: (i, 0)),
        scratch_shapes=[pltpu.VMEM((tm, D_MODEL), BF16)],
        compiler_params=_params("parallel"),
        name="post_mixer_xattn",
    )(x, a, w_mix_out, g, w_q, kv, w_o)


def _gelu_tanh(x):
    return x * (0.5 * (1.0 + jnp.tanh(math.sqrt(2.0 / math.pi) * (x + 0.044715 * (x * x * x)))))


def _ffn_kernel(x_ref, prev_ref, next_ref, g_ref, wup_ref, cw_ref, cb_ref, wdn_ref, gf_ref, o_ref,
                h_ref, *, tm, per_seq, final_norm):
    i = pl.program_id(0)
    x = x_ref[...]
    halo = SUBLANES
    xcat = jnp.concatenate([prev_ref[...], x, next_ref[...]], axis=0)
    xn = _rms(xcat, g_ref[...], NORM_EPS).astype(BF16)
    xn_mid = xn[halo:halo + tm]
    row = lax.broadcasted_iota(jnp.int32, (tm, 1), 0)
    keep_prev = jnp.logical_or(row > 0, i % per_seq != 0)
    keep_next = jnp.logical_or(row < tm - 1, i % per_seq != per_seq - 1)
    rows = tm + 2 * halo
    for c0 in range(0, D_FF, FFN_CHUNK):
        gate = _dot(xn, wup_ref[:, c0:c0 + FFN_CHUNK])
        val = _dot(xn_mid, wup_ref[:, D_FF + c0:D_FF + c0 + FFN_CHUNK])
        g_prev = jnp.where(keep_prev, pltpu.roll(gate, 1, 0)[halo:halo + tm], 0.0)
        g_next = jnp.where(keep_next, pltpu.roll(gate, rows - 1, 0)[halo:halo + tm], 0.0)
        g_mid = gate[halo:halo + tm]
        cw = cw_ref[:, c0:c0 + FFN_CHUNK]
        conv = g_prev * cw[0:1] + g_mid * cw[1:2] + g_next * cw[2:3] + cb_ref[:, c0:c0 + FFN_CHUNK]
        h_ref[:, c0:c0 + FFN_CHUNK] = (_gelu_tanh(conv) * val).astype(BF16)
    y = x + _dot(h_ref[...], wdn_ref[...])
    if final_norm:
        y = _rms(y, gf_ref[...], NORM_EPS)
    o_ref[...] = y


def _ffn(x, g, w_up, conv_w, conv_b, w_down, g_final, seq, tm, final_norm):
    tokens = x.shape[0]
    per_seq = seq // tm
    groups = tm // SUBLANES
    last_group = tokens // SUBLANES - 1
    return pl.pallas_call(
        functools.partial(_ffn_kernel, tm=tm, per_seq=per_seq, final_norm=final_norm),
        out_shape=jax.ShapeDtypeStruct((tokens, D_MODEL), F32),
        grid=(tokens // tm,),
        in_specs=[
            pl.BlockSpec((tm, D_MODEL), lambda i: (i, 0)),
            pl.BlockSpec((SUBLANES, D_MODEL), lambda i: (jnp.maximum(i * groups - 1, 0), 0)),
            pl.BlockSpec((SUBLANES, D_MODEL), lambda i: (jnp.minimum((i + 1) * groups, last_group), 0)),
            _resident((1, D_MODEL)),
            _resident((D_MODEL, 2 * D_FF)),
            _resident((3, D_FF)),
            _resident((1, D_FF)),
            _resident((D_FF, D_MODEL)),
            _resident((1, D_MODEL)),
        ],
        out_specs=pl.BlockSpec((tm, D_MODEL), lambda i: (i, 0)),
        scratch_shapes=[pltpu.VMEM((tm, D_FF), BF16)],
        compiler_params=_params("parallel"),
        name="ffn_convglu",
    )(x, x, x, g, w_up, conv_w, conv_b, w_down, g_final)


def _rope_tables(seq):
    half = DA_HEAD_DIM // 2
    inv_freq = ROPE_THETA ** (-jnp.arange(0, half, dtype=F32) * 2.0 / DA_HEAD_DIM)
    ang = jnp.arange(seq, dtype=F32)[:, None] * inv_freq[None, :]
    cos = jnp.tile(jnp.cos(ang), (1, 4))
    sin = jnp.sin(ang)
    sin = jnp.concatenate([-sin, -sin, sin, sin], axis=1)
    log2e = math.log2(math.e)
    return jnp.concatenate([cos * log2e, cos], axis=1), jnp.concatenate([sin * log2e, sin], axis=1)


def _dft_tables(n):
    idx = jnp.arange(n, dtype=jnp.int32)
    ang = ((idx[:, None] * idx[None, :]) % n).astype(F32) * (2.0 * math.pi / n)
    return jnp.cos(ang), jnp.sin(ang)


def _seq_dft_tables(seq):
    base = FNET_SEQ_BASE
    radix = seq // base
    assert radix * base == seq and base % radix == 0
    k2 = jnp.arange(base, dtype=jnp.int32)[None, :, None]
    n1 = jnp.arange(radix, dtype=jnp.int32)[:, None, None]
    n2 = jnp.arange(base, dtype=jnp.int32)[None, None, :]
    ang = ((k2 * (n1 + radix * n2)) % seq).astype(F32) * (2.0 * math.pi / seq)
    c, s = jnp.cos(ang), jnp.sin(ang)
    stage1 = jnp.concatenate([jnp.concatenate([c, -s], axis=2), jnp.concatenate([-s, -c], axis=2)], axis=1)
    gc, gs = _dft_tables(radix)
    eye = jnp.eye(base // radix, dtype=F32)
    stage2 = jnp.concatenate([jnp.kron(gc, eye), jnp.kron(gs, eye)], axis=1)
    return stage1.astype(BF16), stage2.astype(BF16)


def _rotary_head_layout(w):
    d_in = w.shape[0]
    w = w.reshape(d_in, DA_HEADS, 2, 2, DA_HEAD_DIM // 2)
    return w.transpose(0, 1, 3, 2, 4).reshape(d_in, D_MODEL)


def _qkv_weight(w_qkv):
    wq, wk, wv = jnp.split(w_qkv, 3, axis=1)
    wq = _rotary_head_layout(wq) * (DA_HEAD_DIM ** -0.5)
    return jnp.concatenate([wq, _rotary_head_layout(wk), wv], axis=1).astype(BF16)


def _trunk(x, mem, p):
    batch, seq, _ = x.shape
    mem_tokens = mem.shape[1]
    tokens = batch * seq
    tm = _token_tile(seq, 1024)
    x = x.reshape(tokens, D_MODEL)
    mem = mem.reshape(batch * mem_tokens, D_MODEL)

    for layer in range(DEPTH):
        j = layer // N_MIXERS
        g_mix = p["norm_mix_g"][layer][None]
        if layer % N_MIXERS == 0:
            lambda_init = 0.8 - 0.6 * math.exp(-0.3 * layer)
            cos, sin = _rope_tables(seq)
            tq = _token_tile(seq, 256)
            block = max(tq, min(seq, ATTN_SCORE_SCRATCH_BYTES // (2 * 4 * seq)) // tq * tq)
            assert seq % block == 0
            qkv = _qkv_rope(x, g_mix, p["w_qkv"][j], cos, sin, seq, tm)
            a = _diff_attention(qkv, p["lq1"][j][None], p["lk1"][j][None], p["lq2"][j][None],
                                p["lk2"][j][None], p["subln_g"][j][None], batch, seq, lambda_init,
                                tq=tq, tk=_token_tile(seq, 256), parts=block // tq)
            w_mix_out = p["attn_w_o"][j]
        else:
            ccos, csin = _dft_tables(FNET_GROUP_DIM)
            ab = _fnet_in(x, g_mix, p["fnet_w_in"][j], jnp.concatenate([ccos, csin], axis=1).astype(BF16),
                          batch, seq, tm)
            a = _seq_dft(ab, *_seq_dft_tables(seq), batch, seq, cols=512)
            w_mix_out = p["fnet_w_out"][j]
        kv = _mem_kv(mem, p["norm_mem_g"][layer][None], p["xattn_w_kv"][layer],
                     _token_tile(batch * mem_tokens, 1024))
        x = _post_mixer(x, a, w_mix_out, p["norm_xattn_g"][layer][None], p["xattn_w_q"][layer], kv,
                        p["xattn_w_o"][layer], seq, mem_tokens, tm)
        x = _ffn(x, p["norm_ffn_g"][layer][None], p["ffn_w_up"][layer], p["ffn_conv_w"][layer],
                 p["ffn_conv_b"][layer][None], p["ffn_w_down"][layer], p["final_norm_g"][None],
                 seq, tm, final_norm=(layer == DEPTH - 1))
    return x.reshape(batch, seq, D_MODEL)


def kernel(x_prompt, x_sample, mem_prompt, mem_sample, norm_mix_g, norm_xattn_g, norm_mem_g, norm_ffn_g, final_norm_g, attn_w_qkv, attn_lambda_q1, attn_lambda_k1, attn_lambda_q2, attn_lambda_k2, attn_subln_g, attn_w_o, fnet_w_in, fnet_w_out, xattn_w_q, xattn_w_kv, xattn_w_o, ffn_w_up, ffn_conv_w, ffn_conv_b, ffn_w_down):
    p = {
        "norm_mix_g": norm_mix_g, "norm_xattn_g": norm_xattn_g, "norm_mem_g": norm_mem_g,
        "norm_ffn_g": norm_ffn_g, "final_norm_g": final_norm_g,
        "w_qkv": jnp.stack([_qkv_weight(w) for w in attn_w_qkv]),
        "lq1": attn_lambda_q1, "lk1": attn_lambda_k1, "lq2": attn_lambda_q2, "lk2": attn_lambda_k2,
        "subln_g": attn_subln_g,
        "attn_w_o": attn_w_o.astype(BF16),
        "fnet_w_in": fnet_w_in.astype(BF16), "fnet_w_out": fnet_w_out.astype(BF16),
        "xattn_w_q": (xattn_w_q * (XA_HEAD_DIM ** -0.5)).astype(BF16),
        "xattn_w_kv": xattn_w_kv.astype(BF16), "xattn_w_o": xattn_w_o.astype(BF16),
        "ffn_w_up": ffn_w_up.astype(BF16), "ffn_conv_w": ffn_conv_w, "ffn_conv_b": ffn_conv_b,
        "ffn_w_down": ffn_w_down.astype(BF16),
    }
    return (_trunk(x_prompt, mem_prompt, p), _trunk(x_sample, mem_sample, p))
```

```python
import functools
import math

import jax
import jax.numpy as jnp
from jax import lax
from jax.experimental import pallas as pl
from jax.experimental.pallas import tpu as pltpu

D_MODEL = 1024
DEPTH = 2
N_MIXERS = 2
DA_HEADS = 8
DA_HEAD_DIM = 64
DA_V_DIM = 2 * DA_HEAD_DIM
ROPE_THETA = 10000.0
FNET_GROUPS = 4
FNET_GROUP_DIM = D_MODEL // FNET_GROUPS
FNET_SEQ_BASE = 256
XA_HEADS = 4
XA_HEAD_DIM = D_MODEL // XA_HEADS
D_FF = 2816
NORM_EPS = 1e-6
SUBLN_EPS = 1e-5

LANES = 128
SUBLANES = 8
VMEM_LIMIT_BYTES = 56 * 1024 * 1024
FFN_CHUNK = 256
ATTN_SCORE_SCRATCH_BYTES = 32 * 1024 * 1024

F32 = jnp.float32
BF16 = jnp.bfloat16


def _params(*semantics):
    return pltpu.CompilerParams(dimension_semantics=semantics, vmem_limit_bytes=VMEM_LIMIT_BYTES)


def _resident(shape):
    return pl.BlockSpec(shape, lambda *_: (0,) * len(shape), pipeline_mode=pl.Buffered(1))


def _rms(x, g, eps):
    return x * lax.rsqrt(jnp.mean(x * x, axis=-1, keepdims=True) + eps) * g


def _dot(a, b):
    return jnp.dot(a, b, preferred_element_type=F32)


def _token_tile(seq, want):
    tile = min(seq, want)
    assert seq % tile == 0 and tile % SUBLANES == 0
    return tile


def _qkv_kernel(x_ref, g_ref, w_ref, cos_ref, sin_ref, o_ref, *, chunk):
    xn = _rms(x_ref[...], g_ref[...], NORM_EPS).astype(BF16)
    for c0 in range(0, 3 * D_MODEL, chunk):
        y = _dot(xn, w_ref[:, c0:c0 + chunk])
        for j in range(0, chunk, LANES):
            blk = y[:, j:j + LANES]
            part = (c0 + j) // D_MODEL
            if part < 2:
                cos = cos_ref[:, part * LANES:(part + 1) * LANES]
                sin = sin_ref[:, part * LANES:(part + 1) * LANES]
                blk = blk * cos + pltpu.roll(blk, LANES // 2, 1) * sin
            o_ref[:, c0 + j:c0 + j + LANES] = blk.astype(o_ref.dtype)


def _qkv_rope(x, g, w, cos, sin, seq, tm):
    tokens = x.shape[0]
    n_pos = seq // tm
    return pl.pallas_call(
        functools.partial(_qkv_kernel, chunk=512),
        out_shape=jax.ShapeDtypeStruct((tokens, 3 * D_MODEL), BF16),
        grid=(tokens // tm,),
        in_specs=[
            pl.BlockSpec((tm, D_MODEL), lambda i: (i, 0)),
            _resident((1, D_MODEL)),
            _resident((D_MODEL, 3 * D_MODEL)),
            pl.BlockSpec((tm, 2 * LANES), lambda i: (i % n_pos, 0)),
            pl.BlockSpec((tm, 2 * LANES), lambda i: (i % n_pos, 0)),
        ],
        out_specs=pl.BlockSpec((tm, 3 * D_MODEL), lambda i: (i, 0)),
        compiler_params=_params("parallel"),
        name="qkv_rope",
    )(x, g, w, cos, sin)


def _diff_attn_kernel(q_ref, k_ref, v_ref, lq1_ref, lk1_ref, lq2_ref, lk2_ref, g_ref, o_ref,
                      vt_ref, s_ref, m_ref, acc_ref, l_ref, *, tq, tk, seq, n_q, parts, lambda_init):
    t = pl.program_id(0)
    cols = 2 * tq
    sub = SUBLANES

    @pl.when(t == 0)
    def _():
        s_ref[...] = jnp.zeros(s_ref.shape, F32)
        m_ref[...] = jnp.zeros(m_ref.shape, F32)
        acc_ref[...] = jnp.zeros(acc_ref.shape, F32)
        l_ref[...] = jnp.ones(l_ref.shape, F32)

    @pl.when(jnp.maximum(t - 1, 0) % n_q == 0)
    def _():
        vt_ref[...] = v_ref[...].astype(F32).T.astype(BF16)

    lam = (jnp.exp(jnp.sum(lq1_ref[...] * lk1_ref[...], axis=-1, keepdims=True))
           - jnp.exp(jnp.sum(lq2_ref[...] * lk2_ref[...], axis=-1, keepdims=True))
           + lambda_init)

    for part in range(parts):
        rows = slice(part * tq, (part + 1) * tq)
        span = slice(part * cols, (part + 1) * cols)

        q = q_ref[rows, :]
        lane = lax.broadcasted_iota(jnp.int32, q.shape, 1)
        first = (lane // (DA_HEAD_DIM // 2)) % 2 == 0
        zero = jnp.zeros_like(q)
        qs = jnp.concatenate([jnp.where(first, q, zero), jnp.where(first, zero, q)], axis=0)

        o = acc_ref[:, span] / jnp.sum(l_ref[:, span], axis=0, keepdims=True)
        o = o[:, :tq] - lam * o[:, tq:]
        o = o * lax.rsqrt(jnp.mean(o * o, axis=0, keepdims=True) + SUBLN_EPS) * g_ref[...] * (1.0 - lambda_init)
        o_ref[rows, :] = o.T.astype(o_ref.dtype)

        m_old = m_ref[:, span]
        mx = lsum = acc = None
        for off in range(0, seq, tk):
            s_new = lax.dot_general(k_ref[off:off + tk, :], qs, (((1,), (1,)), ((), ())),
                                    preferred_element_type=F32)
            s_old = s_ref[off:off + tk, span]
            chunk_max = jnp.max(s_new.reshape(tk // sub, sub, cols), axis=0)
            mx = chunk_max if mx is None else jnp.maximum(mx, chunk_max)
            p = jnp.exp2(s_old.reshape(tk // sub, sub, cols) - m_old)
            chunk_sum = jnp.sum(p, axis=0)
            lsum = chunk_sum if lsum is None else lsum + chunk_sum
            s_ref[off:off + tk, span] = s_new
            pv = _dot(vt_ref[:, off:off + tk], p.reshape(tk, cols).astype(BF16))
            acc = pv if acc is None else acc + pv

        acc_ref[:, span] = acc
        l_ref[:, span] = lsum
        m_ref[:, span] = jnp.broadcast_to(jnp.max(mx, axis=0, keepdims=True), (sub, cols))


def _diff_attention(qkv, lq1, lk1, lq2, lk2, subln_g, batch, seq, lambda_init, tq, tk, parts):
    block = parts * tq
    assert seq % block == 0
    n_q = seq // block
    n_tiles = batch * DA_HEADS * n_q

    def stage_tile(t, lag):
        return jnp.clip(t - lag, 0, n_tiles - 1)

    def scored(t):
        return stage_tile(t, 0)

    def applied(t):
        return stage_tile(t, 1)

    def written(t):
        return stage_tile(t, 2)

    def head_of(tile):
        return tile // (DA_HEADS * n_q), (tile // n_q) % DA_HEADS

    def q_block(tile):
        b, h = head_of(tile)
        return b * n_q + tile % n_q, h

    def k_block(t):
        b, h = head_of(scored(t))
        return b, DA_HEADS + h

    def v_block(t):
        b, h = head_of(applied(t))
        return b, 2 * DA_HEADS + h

    vec = _resident((1, DA_HEAD_DIM))
    cols = 2 * block
    return pl.pallas_call(
        functools.partial(_diff_attn_kernel, tq=tq, tk=tk, seq=seq, n_q=n_q, parts=parts,
                          lambda_init=lambda_init),
        out_shape=jax.ShapeDtypeStruct((batch * seq, D_MODEL), BF16),
        grid=(n_tiles + 2,),
        in_specs=[
            pl.BlockSpec((block, DA_V_DIM), lambda t: q_block(scored(t))),
            pl.BlockSpec((seq, DA_V_DIM), k_block),
            pl.BlockSpec((seq, DA_V_DIM), v_block),
            vec, vec, vec, vec,
            _resident((DA_V_DIM, 1)),
        ],
        out_specs=pl.BlockSpec((block, DA_V_DIM), lambda t: q_block(written(t))),
        scratch_shapes=[
            pltpu.VMEM((DA_V_DIM, seq), BF16),
            pltpu.VMEM((seq, cols), F32),
            pltpu.VMEM((SUBLANES, cols), F32),
            pltpu.VMEM((DA_V_DIM, cols), F32),
            pltpu.VMEM((SUBLANES, cols), F32),
        ],
        compiler_params=_params("arbitrary"),
        name="diff_attention",
    )(qkv, qkv, qkv, lq1, lk1, lq2, lk2, subln_g.reshape(DA_V_DIM, 1))


def _fnet_in_kernel(x_ref, g_ref, w_ref, dft_ref, o_ref, *, radix, per_residue):
    x = jnp.concatenate([x_ref[:, n1, :] for n1 in range(radix)], axis=0)
    xn = _rms(x, g_ref[...], NORM_EPS).astype(BF16)
    u = _dot(xn, w_ref[...]).astype(BF16)
    dft = dft_ref[...]
    for grp in range(FNET_GROUPS):
        lo = grp * FNET_GROUP_DIM
        ab = _dot(u[:, lo:lo + FNET_GROUP_DIM], dft).astype(o_ref.dtype)
        for n1 in range(radix):
            rows = slice(n1 * per_residue, (n1 + 1) * per_residue)
            o_ref[n1, :, lo:lo + FNET_GROUP_DIM] = ab[rows, :FNET_GROUP_DIM]
            o_ref[n1, :, D_MODEL + lo:D_MODEL + lo + FNET_GROUP_DIM] = ab[rows, FNET_GROUP_DIM:]


def _fnet_in(x, g, w, dft, batch, seq, tm):
    radix = seq // FNET_SEQ_BASE
    per_residue = tm // radix
    tiles_per_seq = seq // tm
    return pl.pallas_call(
        functools.partial(_fnet_in_kernel, radix=radix, per_residue=per_residue),
        out_shape=jax.ShapeDtypeStruct((batch * radix, FNET_SEQ_BASE, 2 * D_MODEL), BF16),
        grid=(batch * tiles_per_seq,),
        in_specs=[
            pl.BlockSpec((per_residue, radix, D_MODEL), lambda i: (i, 0, 0)),
            _resident((1, D_MODEL)),
            _resident((D_MODEL, D_MODEL)),
            _resident((FNET_GROUP_DIM, 2 * FNET_GROUP_DIM)),
        ],
        out_specs=pl.BlockSpec((radix, per_residue, 2 * D_MODEL),
                               lambda i: (i // tiles_per_seq, i % tiles_per_seq, 0)),
        compiler_params=_params("parallel"),
        name="fnet_in",
    )(x.reshape(x.shape[0] // radix, radix, D_MODEL), g, w, dft)


def _seq_dft_kernel(a_ref, b_ref, stage1_ref, stage2_ref, o_ref, y_ref, *, radix, scale):
    base = FNET_SEQ_BASE
    for n1 in range(radix):
        m = stage1_ref[n1]
        y = (_dot(m[:, :base], a_ref[n1]) + _dot(m[:, base:], b_ref[n1])).astype(BF16)
        y_ref[0, n1] = y[:base]
        y_ref[1, n1] = y[base:]
    kb = base // radix
    g = stage2_ref[...]
    cols = y_ref.shape[-1]
    for k2 in range(0, base, kb):
        yr = y_ref[0, :, k2:k2 + kb, :].reshape(base, cols)
        yi = y_ref[1, :, k2:k2 + kb, :].reshape(base, cols)
        out = (_dot(g[:, :base], yr) + _dot(g[:, base:], yi)) * scale
        o_ref[:, k2:k2 + kb, :] = out.reshape(radix, kb, cols).astype(o_ref.dtype)


def _seq_dft(ab, stage1, stage2, batch, seq, cols):
    radix = seq // FNET_SEQ_BASE
    col_blocks = D_MODEL // cols
    scale = 1.0 / math.sqrt(seq * FNET_GROUP_DIM)
    out = pl.pallas_call(
        functools.partial(_seq_dft_kernel, radix=radix, scale=scale),
        out_shape=jax.ShapeDtypeStruct((batch * radix, FNET_SEQ_BASE, D_MODEL), BF16),
        grid=(batch, col_blocks),
        in_specs=[
            pl.BlockSpec((radix, FNET_SEQ_BASE, cols), lambda b, c: (b, 0, c)),
            pl.BlockSpec((radix, FNET_SEQ_BASE, cols), lambda b, c: (b, 0, col_blocks + c)),
            _resident((radix, 2 * FNET_SEQ_BASE, 2 * FNET_SEQ_BASE)),
            _resident((FNET_SEQ_BASE, 2 * FNET_SEQ_BASE)),
        ],
        out_specs=pl.BlockSpec((radix, FNET_SEQ_BASE, cols), lambda b, c: (b, 0, c)),
        scratch_shapes=[pltpu.VMEM((2, radix, FNET_SEQ_BASE, cols), BF16)],
        compiler_params=_params("parallel", "parallel"),
        name="seq_dft",
    )(ab, ab, stage1, stage2)
    return out.reshape(batch * seq, D_MODEL)


def _post_mixer_kernel(x_ref, a_ref, wm_ref, g_ref, wq_ref, mem_ref, gm_ref, wkv_ref, wo_ref, o_ref,
                       heads_ref, kv_ref, *, per_seq):
    @pl.when(pl.program_id(0) % per_seq == 0)
    def _():
        mn = _rms(mem_ref[...], gm_ref[...], NORM_EPS).astype(BF16)
        kv_ref[...] = _dot(mn, wkv_ref[...]).astype(BF16)

    x1 = x_ref[...] + _dot(a_ref[...], wm_ref[...])
    xn = _rms(x1, g_ref[...], NORM_EPS).astype(BF16)
    q = _dot(xn, wq_ref[...]).astype(BF16)
    for h in range(XA_HEADS):
        lo = h * XA_HEAD_DIM
        kh = kv_ref[:, lo:lo + XA_HEAD_DIM]
        vh = kv_ref[:, D_MODEL + lo:D_MODEL + lo + XA_HEAD_DIM]
        s = lax.dot_general(q[:, lo:lo + XA_HEAD_DIM], kh, (((1,), (1,)), ((), ())),
                            preferred_element_type=F32)
        p = jnp.exp(s - jnp.max(s, axis=-1, keepdims=True))
        denom = jnp.sum(p, axis=-1, keepdims=True)
        heads_ref[:, lo:lo + XA_HEAD_DIM] = (_dot(p.astype(BF16), vh) / denom).astype(BF16)
    o_ref[...] = x1 + _dot(heads_ref[...], wo_ref[...])


def _post_mixer(x, a, w_mix_out, g, w_q, mem, g_mem, w_kv, w_o, seq, mem_tokens, tm):
    tokens = x.shape[0]
    per_seq = seq // tm
    square = _resident((D_MODEL, D_MODEL))
    return pl.pallas_call(
        functools.partial(_post_mixer_kernel, per_seq=per_seq),
        out_shape=jax.ShapeDtypeStruct((tokens, D_MODEL), F32),
        grid=(tokens // tm,),
        in_specs=[
            pl.BlockSpec((tm, D_MODEL), lambda i: (i, 0)),
            pl.BlockSpec((tm, D_MODEL), lambda i: (i, 0)),
            square,
            _resident((1, D_MODEL)),
            square,
            pl.BlockSpec((mem_tokens, D_MODEL), lambda i: (i // per_seq, 0)),
            _resident((1, D_MODEL)),
            _resident((D_MODEL, 2 * D_MODEL)),
            square,
        ],
        out_specs=pl.BlockSpec((tm, D_MODEL), lambda i: (i, 0)),
        scratch_shapes=[
            pltpu.VMEM((tm, D_MODEL), BF16),
            pltpu.VMEM((mem_tokens, 2 * D_MODEL), BF16),
        ],
        compiler_params=_params("arbitrary"),
        name="post_mixer_xattn",
    )(x, a, w_mix_out, g, w_q, mem, g_mem, w_kv, w_o)


def _gelu_tanh(x):
    return x * (0.5 * (1.0 + jnp.tanh(math.sqrt(2.0 / math.pi) * (x + 0.044715 * (x * x * x)))))


def _ffn_kernel(x_ref, prev_ref, next_ref, g_ref, wup_ref, cw_ref, cb_ref, wdn_ref, gf_ref, o_ref,
                h_ref, *, tm, per_seq, final_norm):
    i = pl.program_id(0)
    x = x_ref[...]
    halo = SUBLANES
    xcat = jnp.concatenate([prev_ref[...], x, next_ref[...]], axis=0)
    xn = _rms(xcat, g_ref[...], NORM_EPS).astype(BF16)
    xn_mid = xn[halo:halo + tm]
    row = lax.broadcasted_iota(jnp.int32, (tm, 1), 0)
    keep_prev = jnp.logical_or(row > 0, i % per_seq != 0)
    keep_next = jnp.logical_or(row < tm - 1, i % per_seq != per_seq - 1)
    rows = tm + 2 * halo
    for c0 in range(0, D_FF, FFN_CHUNK):
        gate = _dot(xn, wup_ref[:, c0:c0 + FFN_CHUNK])
        val = _dot(xn_mid, wup_ref[:, D_FF + c0:D_FF + c0 + FFN_CHUNK])
        g_prev = jnp.where(keep_prev, pltpu.roll(gate, 1, 0)[halo:halo + tm], 0.0)
        g_next = jnp.where(keep_next, pltpu.roll(gate, rows - 1, 0)[halo:halo + tm], 0.0)
        g_mid = gate[halo:halo + tm]
        cw = cw_ref[:, c0:c0 + FFN_CHUNK]
        conv = g_prev * cw[0:1] + g_mid * cw[1:2] + g_next * cw[2:3] + cb_ref[:, c0:c0 + FFN_CHUNK]
        h_ref[:, c0:c0 + FFN_CHUNK] = (_gelu_tanh(conv) * val).astype(BF16)
    y = x + _dot(h_ref[...], wdn_ref[...])
    if final_norm:
        y = _rms(y, gf_ref[...], NORM_EPS)
    o_ref[...] = y


def _ffn(x, g, w_up, conv_w, conv_b, w_down, g_final, seq, tm, final_norm):
    tokens = x.shape[0]
    per_seq = seq // tm
    groups = tm // SUBLANES
    last_group = tokens // SUBLANES - 1
    return pl.pallas_call(
        functools.partial(_ffn_kernel, tm=tm, per_seq=per_seq, final_norm=final_norm),
        out_shape=jax.ShapeDtypeStruct((tokens, D_MODEL), F32),
        grid=(tokens // tm,),
        in_specs=[
            pl.BlockSpec((tm, D_MODEL), lambda i: (i, 0)),
            pl.BlockSpec((SUBLANES, D_MODEL), lambda i: (jnp.maximum(i * groups - 1, 0), 0)),
            pl.BlockSpec((SUBLANES, D_MODEL), lambda i: (jnp.minimum((i + 1) * groups, last_group), 0)),
            _resident((1, D_MODEL)),
            _resident((D_MODEL, 2 * D_FF)),
            _resident((3, D_FF)),
            _resident((1, D_FF)),
            _resident((D_FF, D_MODEL)),
            _resident((1, D_MODEL)),
        ],
        out_specs=pl.BlockSpec((tm, D_MODEL), lambda i: (i, 0)),
        scratch_shapes=[pltpu.VMEM((tm, D_FF), BF16)],
        compiler_params=_params("parallel"),
        name="ffn_convglu",
    )(x, x, x, g, w_up, conv_w, conv_b, w_down, g_final)


def _rope_tables(seq):
    half = DA_HEAD_DIM // 2
    inv_freq = ROPE_THETA ** (-jnp.arange(0, half, dtype=F32) * 2.0 / DA_HEAD_DIM)
    ang = jnp.arange(seq, dtype=F32)[:, None] * inv_freq[None, :]
    cos = jnp.tile(jnp.cos(ang), (1, 4))
    sin = jnp.sin(ang)
    sin = jnp.concatenate([-sin, -sin, sin, sin], axis=1)
    log2e = math.log2(math.e)
    return jnp.concatenate([cos * log2e, cos], axis=1), jnp.concatenate([sin * log2e, sin], axis=1)


def _dft_tables(n):
    idx = jnp.arange(n, dtype=jnp.int32)
    ang = ((idx[:, None] * idx[None, :]) % n).astype(F32) * (2.0 * math.pi / n)
    return jnp.cos(ang), jnp.sin(ang)


def _seq_dft_tables(seq):
    base = FNET_SEQ_BASE
    radix = seq // base
    assert radix * base == seq and base % radix == 0
    k2 = jnp.arange(base, dtype=jnp.int32)[None, :, None]
    n1 = jnp.arange(radix, dtype=jnp.int32)[:, None, None]
    n2 = jnp.arange(base, dtype=jnp.int32)[None, None, :]
    ang = ((k2 * (n1 + radix * n2)) % seq).astype(F32) * (2.0 * math.pi / seq)
    c, s = jnp.cos(ang), jnp.sin(ang)
    stage1 = jnp.concatenate([jnp.concatenate([c, -s], axis=2), jnp.concatenate([-s, -c], axis=2)], axis=1)
    gc, gs = _dft_tables(radix)
    eye = jnp.eye(base // radix, dtype=F32)
    stage2 = jnp.concatenate([jnp.kron(gc, eye), jnp.kron(gs, eye)], axis=1)
    return stage1.astype(BF16), stage2.astype(BF16)


def _rotary_head_layout(w):
    d_in = w.shape[0]
    w = w.reshape(d_in, DA_HEADS, 2, 2, DA_HEAD_DIM // 2)
    return w.transpose(0, 1, 3, 2, 4).reshape(d_in, D_MODEL)


def _qkv_weight(w_qkv):
    wq, wk, wv = jnp.split(w_qkv, 3, axis=1)
    wq = _rotary_head_layout(wq) * (DA_HEAD_DIM ** -0.5)
    return jnp.concatenate([wq, _rotary_head_layout(wk), wv], axis=1).astype(BF16)


def _trunk(x, mem, p):
    batch, seq, _ = x.shape
    mem_tokens = mem.shape[1]
    tokens = batch * seq
    tm = _token_tile(seq, 1024)
    x = x.reshape(tokens, D_MODEL)
    mem = mem.reshape(batch * mem_tokens, D_MODEL)

    for layer in range(DEPTH):
        j = layer // N_MIXERS
        g_mix = p["norm_mix_g"][layer][None]
        if layer % N_MIXERS == 0:
            lambda_init = 0.8 - 0.6 * math.exp(-0.3 * layer)
            cos, sin = _rope_tables(seq)
            tq = _token_tile(seq, 256)
            block = max(tq, min(seq, ATTN_SCORE_SCRATCH_BYTES // (2 * 4 * seq)) // tq * tq)
            assert seq % block == 0
            qkv = _qkv_rope(x, g_mix, p["w_qkv"][j], cos, sin, seq, tm)
            a = _diff_attention(qkv, p["lq1"][j][None], p["lk1"][j][None], p["lq2"][j][None],
                                p["lk2"][j][None], p["subln_g"][j][None], batch, seq, lambda_init,
                                tq=tq, tk=_token_tile(seq, 256), parts=block // tq)
            w_mix_out = p["attn_w_o"][j]
        else:
            ccos, csin = _dft_tables(FNET_GROUP_DIM)
            ab = _fnet_in(x, g_mix, p["fnet_w_in"][j], jnp.concatenate([ccos, csin], axis=1).astype(BF16),
                          batch, seq, tm)
            a = _seq_dft(ab, *_seq_dft_tables(seq), batch, seq, cols=512)
            w_mix_out = p["fnet_w_out"][j]
        x = _post_mixer(x, a, w_mix_out, p["norm_xattn_g"][layer][None], p["xattn_w_q"][layer], mem,
                        p["norm_mem_g"][layer][None], p["xattn_w_kv"][layer], p["xattn_w_o"][layer],
                        seq, mem_tokens, tm)
        x = _ffn(x, p["norm_ffn_g"][layer][None], p["ffn_w_up"][layer], p["ffn_conv_w"][layer],
                 p["ffn_conv_b"][layer][None], p["ffn_w_down"][layer], p["final_norm_g"][None],
                 seq, tm, final_norm=(layer == DEPTH - 1))
    return x.reshape(batch, seq, D_MODEL)


def kernel(x_prompt, x_sample, mem_prompt, mem_sample, norm_mix_g, norm_xattn_g, norm_mem_g, norm_ffn_g, final_norm_g, attn_w_qkv, attn_lambda_q1, attn_lambda_k1, attn_lambda_q2, attn_lambda_k2, attn_subln_g, attn_w_o, fnet_w_in, fnet_w_out, xattn_w_q, xattn_w_kv, xattn_w_o, ffn_w_up, ffn_conv_w, ffn_conv_b, ffn_w_down):
    p = {
        "norm_mix_g": norm_mix_g, "norm_xattn_g": norm_xattn_g, "norm_mem_g": norm_mem_g,
        "norm_ffn_g": norm_ffn_g, "final_norm_g": final_norm_g,
        "w_qkv": jnp.stack([_qkv_weight(w) for w in attn_w_qkv]),
        "lq1": attn_lambda_q1, "lk1": attn_lambda_k1, "lq2": attn_lambda_q2, "lk2": attn_lambda_k2,
        "subln_g": attn_subln_g,
        "attn_w_o": attn_w_o.astype(BF16),
        "fnet_w_in": fnet_w_in.astype(BF16), "fnet_w_out": fnet_w_out.astype(BF16),
        "xattn_w_q": (xattn_w_q * (XA_HEAD_DIM ** -0.5)).astype(BF16),
        "xattn_w_kv": xattn_w_kv.astype(BF16), "xattn_w_o": xattn_w_o.astype(BF16),
        "ffn_w_up": ffn_w_up.astype(BF16), "ffn_conv_w": ffn_conv_w, "ffn_conv_b": ffn_conv_b,
        "ffn_w_down": ffn_w_down.astype(BF16),
    }
    return (_trunk(x_prompt, mem_prompt, p), _trunk(x_sample, mem_sample, p))
```
